```python
import math
import jax, jax.numpy as jnp
from jax import lax
import numpy as np

D_MODEL = 1024
BATCH = 8
SEQ = 2048
DEPTH = 4

CHUNK = 64
N_MIXERS = 3
D_FF = 4 * D_MODEL
NORM_EPS = 1e-6
GDN_HEADS = 8
GDN_DK = D_MODEL // GDN_HEADS
GDN_DV = D_MODEL // GDN_HEADS
GDN_CONV = 4
GDN_QKV = GDN_HEADS * (2 * GDN_DK + GDN_DV)
GDN_IN = GDN_QKV + GDN_HEADS * GDN_DV + 2 * GDN_HEADS
MLSTM_HEADS = 8
MLSTM_DV = D_MODEL // MLSTM_HEADS
MLSTM_DK = MLSTM_DV // 2
GATE_CAP = 15.0
MLSTM_IN = MLSTM_HEADS * (2 * MLSTM_DK + 2 * MLSTM_DV) + 2 * MLSTM_HEADS
DIFF_HEADS = 8
DIFF_D = D_MODEL // (2 * DIFF_HEADS)
ROPE_THETA = 500000.0
ROT_DIMS = DIFF_D // 4
Q_BLOCK = 128
N_GDN = (DEPTH + 2) // 3
N_MLSTM = (DEPTH + 1) // 3
N_DIFF = DEPTH // 3

kernel_name = "hybrid_gdn_mlstm_diffattn_trunk"


def rms_norm(x, g):
    xf = x.astype(jnp.float32)
    y = xf * lax.rsqrt(jnp.mean(xf * xf, axis=-1, keepdims=True) + NORM_EPS)
    return (y * g.astype(jnp.float32)).astype(x.dtype)


def l2_norm(x):
    xf = x.astype(jnp.float32)
    return xf * lax.rsqrt(jnp.sum(xf * xf, axis=-1, keepdims=True) + NORM_EPS)


def soft_cap(x):
    return GATE_CAP * jnp.tanh(x / GATE_CAP)


def to_chunks(x):
    b, t, h = x.shape[:3]
    x = x.reshape((b, t // CHUNK, CHUNK, h) + x.shape[3:])
    return jnp.swapaxes(jnp.moveaxis(x, 1, 0), 2, 3)


def from_chunks(x):
    n, b, h, c, d = x.shape
    return jnp.moveaxis(jnp.swapaxes(x, 2, 3), 0, 1).reshape(b, n * c, h, d)


def causal_conv(x, w):
    width, t = w.shape[0], x.shape[1]
    xp = jnp.pad(x, ((0, 0), (width - 1, 0), (0, 0)))
    return sum(xp[:, j:j + t] * w[j] for j in range(width))


def partial_rope(x, cos, sin):
    shp = (1, x.shape[1]) + (1,) * (x.ndim - 3) + (ROT_DIMS // 2,)
    c, s = cos.reshape(shp), sin.reshape(shp)
    x1, x2, rest = x[..., :ROT_DIMS // 2], x[..., ROT_DIMS // 2:ROT_DIMS], x[..., ROT_DIMS:]
    return jnp.concatenate([x1 * c - x2 * s, x1 * s + x2 * c, rest], axis=-1)


def gated_delta_rule(q, k, v, g, beta):
    dk, dv = q.shape[-1], v.shape[-1]
    qc = to_chunks(q * dk ** -0.5)
    kc = to_chunks(k)
    vc = to_chunks(v.astype(jnp.float32))
    gc = jnp.cumsum(to_chunks(g.astype(jnp.float32)), axis=-1)
    bc = to_chunks(beta.astype(jnp.float32))
    tril = jnp.tril(jnp.ones((CHUNK, CHUNK), bool))
    strict = jnp.tril(jnp.ones((CHUNK, CHUNK), bool), -1)
    diff = gc[..., :, None] - gc[..., None, :]
    decay = jnp.where(tril, jnp.exp(jnp.where(tril, diff, 0.0)), 0.0)
    kb = kc * bc[..., None]
    a_mat = jnp.where(strict, jnp.einsum('nbhid,nbhjd->nbhij', kb, kc) * decay, 0.0)
    t_mat = a_mat + jnp.eye(CHUNK, dtype=jnp.float32)
    rhs = jnp.concatenate([vc * bc[..., None], kb * jnp.exp(gc)[..., None]], axis=-1)
    sol = lax.linalg.triangular_solve(t_mat, rhs, left_side=True, lower=True, unit_diagonal=True)
    value, kcd = sol[..., :dv], sol[..., dv:]
    attn = jnp.einsum('nbhid,nbhjd->nbhij', qc, kc) * decay
    q_inter = qc * jnp.exp(gc)[..., None]
    g_last = gc[..., -1]
    k_state = kc * jnp.exp(g_last[..., None] - gc)[..., None]

    def step(s_st, xs):
        qi, kcd_i, val_i, attn_i, kst_i, gl_i = xs
        v_new = val_i - jnp.einsum('bhck,bhkv->bhcv', kcd_i, s_st)
        o = jnp.einsum('bhck,bhkv->bhcv', qi, s_st) + jnp.einsum('bhcs,bhsv->bhcv', attn_i, v_new)
        s_st = s_st * jnp.exp(gl_i)[..., None, None] + jnp.einsum('bhck,bhcv->bhkv', kst_i, v_new)
        return s_st, o

    s0 = jnp.zeros(qc.shape[1:3] + (dk, dv), jnp.float32)
    _, o = lax.scan(step, s0, (q_inter, kcd, value, attn, k_state, g_last))
    return from_chunks(o)


def gdn_mixer(h, w_in, conv_w, a_log, dt_bias, norm_g, w_out):
    b, t, _ = h.shape
    proj = h @ w_in
    qkv = jax.nn.silu(causal_conv(proj[..., :GDN_QKV], conv_w))
    z = proj[..., GDN_QKV:GDN_QKV + GDN_HEADS * GDN_DV].reshape(b, t, GDN_HEADS, GDN_DV)
    a = proj[..., GDN_QKV + GDN_HEADS * GDN_DV:GDN_IN - GDN_HEADS]
    bt = proj[..., GDN_IN - GDN_HEADS:]
    hk = GDN_HEADS * GDN_DK
    q = l2_norm(qkv[..., :hk].reshape(b, t, GDN_HEADS, GDN_DK))
    k = l2_norm(qkv[..., hk:2 * hk].reshape(b, t, GDN_HEADS, GDN_DK))
    v = qkv[..., 2 * hk:].reshape(b, t, GDN_HEADS, GDN_DV)
    g = -jnp.exp(a_log.astype(jnp.float32)) * jax.nn.softplus((a + dt_bias).astype(jnp.float32))
    beta = jax.nn.sigmoid(bt.astype(jnp.float32))
    o = gated_delta_rule(q, k, v, g, beta)
    o = rms_norm(o, norm_g) * jax.nn.silu(z.astype(jnp.float32))
    return o.reshape(b, t, GDN_HEADS * GDN_DV).astype(h.dtype) @ w_out


def mlstm_chunkwise(q, k, v, i_pre, f_pre):
    dk, dv = q.shape[-1], v.shape[-1]
    qc = to_chunks(q.astype(jnp.float32) * dk ** -0.5)
    kc = to_chunks(k.astype(jnp.float32))
    vc = to_chunks(v.astype(jnp.float32))
    lfc = to_chunks(jax.nn.log_sigmoid(f_pre.astype(jnp.float32)))
    ic = to_chunks(i_pre.astype(jnp.float32))
    tril = jnp.tril(jnp.ones((CHUNK, CHUNK), bool))

    def step(carry, xs):
        c_st, n_st, m_st = carry
        q_, k_, v_, lf, ig = xs
        bcum = jnp.cumsum(lf, axis=-1)
        g = bcum[..., -1]
        dlog = jnp.where(tril, bcum[..., :, None] - bcum[..., None, :] + ig[..., None, :], -jnp.inf)
        m_inter = bcum + m_st[..., None]
        m_out = jnp.maximum(m_inter, jnp.max(dlog, axis=-1))
        w_inter = jnp.exp(m_inter - m_out)
        s = jnp.einsum('bhtd,bhsd->bhts', q_, k_) * jnp.exp(dlog - m_out[..., None])
        num = w_inter[..., None] * jnp.einsum('bhtd,bhde->bhte', q_, c_st) + jnp.einsum('bhts,bhse->bhte', s, v_)
        den = w_inter * jnp.einsum('bhtd,bhd->bht', q_, n_st) + jnp.sum(s, axis=-1)
        h_ = num / jnp.maximum(jnp.abs(den), jnp.exp(-m_out))[..., None]
        a = g[..., None] - bcum + ig
        m_new = jnp.maximum(g + m_st, jnp.max(a, axis=-1))
        w_old = jnp.exp(g + m_st - m_new)
        kw = k_ * jnp.exp(a - m_new[..., None])[..., None]
        c_new = w_old[..., None, None] * c_st + jnp.einsum('bhsd,bhse->bhde', kw, v_)
        n_new = w_old[..., None] * n_st + jnp.sum(kw, axis=-2)
        return (c_new, n_new, m_new), h_

    bh = qc.shape[1:3]
    init = (jnp.zeros(bh + (dk, dv), jnp.float32), jnp.zeros(bh + (dk,), jnp.float32),
            jnp.zeros(bh, jnp.float32))
    _, hs = lax.scan(step, init, (qc, kc, vc, lfc, ic))
    return from_chunks(hs)


def mlstm_mixer(h, w_in, gate_b, norm_g, w_out):
    b, t, _ = h.shape
    proj = h @ w_in
    hk, hv = MLSTM_HEADS * MLSTM_DK, MLSTM_HEADS * MLSTM_DV
    q = proj[..., :hk].reshape(b, t, MLSTM_HEADS, MLSTM_DK)
    k = proj[..., hk:2 * hk].reshape(b, t, MLSTM_HEADS, MLSTM_DK)
    v = proj[..., 2 * hk:2 * hk + hv].reshape(b, t, MLSTM_HEADS, MLSTM_DV)
    o_gate = proj[..., 2 * hk + hv:2 * hk + 2 * hv]
    i_pre = soft_cap(proj[..., MLSTM_IN - 2 * MLSTM_HEADS:MLSTM_IN - MLSTM_HEADS] + gate_b[0])
    f_pre = soft_cap(proj[..., MLSTM_IN - MLSTM_HEADS:] + gate_b[1])
    hc = mlstm_chunkwise(q, k, v, i_pre, f_pre)
    hn = rms_norm(hc, norm_g.reshape(MLSTM_HEADS, MLSTM_DV)).reshape(b, t, hv)
    return (jax.nn.sigmoid(o_gate.astype(jnp.float32)) * hn).astype(h.dtype) @ w_out


def diff_attention(q, k, v, lam):
    b, t, h, _, d = q.shape
    nq = t // Q_BLOCK
    qf = jnp.moveaxis((q.astype(jnp.float32) * d ** -0.5).reshape(b, nq, Q_BLOCK, h, 2, d), 1, 0)
    kf, vf = k.astype(jnp.float32), v.astype(jnp.float32)
    key_chunk = jnp.arange(t) // CHUNK

    def one_block(args):
        qb, idx = args
        q_chunk = (idx * Q_BLOCK + jnp.arange(Q_BLOCK)) // CHUNK
        s = jnp.einsum('bqhcd,bkhcd->bhcqk', qb, kf)
        mask = key_chunk[None, :] <= q_chunk[:, None]
        p = jax.nn.softmax(jnp.where(mask, s, -jnp.inf), axis=-1)
        attn = p[:, :, 0] - lam * p[:, :, 1]
        return jnp.einsum('bhqk,bkhe->bqhe', attn, vf)

    o = lax.map(one_block, (qf, jnp.arange(nq)))
    return jnp.moveaxis(o, 0, 1).reshape(b, t, h, 2 * d)


def diff_mixer(h, w_in, lam_p, norm_g, w_out, lambda_init, cos, sin):
    b, t, _ = h.shape
    proj = h @ w_in
    hq = DIFF_HEADS * 2 * DIFF_D
    q = partial_rope(proj[..., :hq].reshape(b, t, DIFF_HEADS, 2, DIFF_D), cos, sin)
    k = partial_rope(proj[..., hq:2 * hq].reshape(b, t, DIFF_HEADS, 2, DIFF_D), cos, sin)
    v = proj[..., 2 * hq:].reshape(b, t, DIFF_HEADS, 2 * DIFF_D)
    lp = lam_p.astype(jnp.float32)
    lam = jnp.exp(jnp.sum(lp[0] * lp[1])) - jnp.exp(jnp.sum(lp[2] * lp[3])) + lambda_init
    o = diff_attention(q, k, v, lam)
    o = rms_norm(o, norm_g) * (1.0 - lambda_init)
    return o.reshape(b, t, hq).astype(h.dtype) @ w_out


def sq_relu_mlp(h, w1, w2):
    return jnp.square(jax.nn.relu(h @ w1)) @ w2


def setup_inputs(seed: int = 0) -> dict:
    key = jax.random.key(seed)
    ks = jax.random.split(key, 18)
    nrm = lambda k, shape, scale: jax.random.normal(k, shape, jnp.float32) * scale
    return {
        "x": nrm(ks[0], (BATCH, SEQ, D_MODEL), 1.0),
        "norm_g": 1.0 + nrm(ks[1], (DEPTH, 4, D_MODEL), 0.05),
        "mlp_w1": nrm(ks[2], (DEPTH, D_MODEL, D_FF), D_MODEL ** -0.5),
        "mlp_w2": nrm(ks[3], (DEPTH, D_FF, D_MODEL), D_FF ** -0.5),
        "gdn_w_in": nrm(ks[4], (N_GDN, D_MODEL, GDN_IN), D_MODEL ** -0.5),
        "gdn_conv": nrm(ks[5], (N_GDN, GDN_CONV, GDN_QKV), GDN_CONV ** -0.5),
        "gdn_a_log": jnp.log(jax.random.uniform(ks[6], (N_GDN, GDN_HEADS), jnp.float32, 1.0, 16.0)),
        "gdn_dt_bias": -4.0 + nrm(ks[7], (N_GDN, GDN_HEADS), 0.5),
        "gdn_norm_g": 1.0 + nrm(ks[8], (N_GDN, GDN_DV), 0.05),
        "gdn_w_out": nrm(ks[9], (N_GDN, GDN_HEADS * GDN_DV, D_MODEL), (GDN_HEADS * GDN_DV) ** -0.5),
        "mlstm_w_in": nrm(ks[10], (N_MLSTM, D_MODEL, MLSTM_IN), D_MODEL ** -0.5),
        "mlstm_gate_b": jnp.array([[-1.0], [3.0]], jnp.float32) + nrm(ks[11], (N_MLSTM, 2, MLSTM_HEADS), 0.1),
        "mlstm_norm_g": 1.0 + nrm(ks[12], (N_MLSTM, MLSTM_HEADS * MLSTM_DV), 0.05),
        "mlstm_w_out": nrm(ks[13], (N_MLSTM, MLSTM_HEADS * MLSTM_DV, D_MODEL), (MLSTM_HEADS * MLSTM_DV) ** -0.5),
        "diff_w_in": nrm(ks[14], (N_DIFF, D_MODEL, 3 * DIFF_HEADS * 2 * DIFF_D), D_MODEL ** -0.5),
        "diff_lambda": nrm(ks[15], (N_DIFF, 4, DIFF_D), 0.1),
        "diff_norm_g": 1.0 + nrm(ks[16], (N_DIFF, 2 * DIFF_D), 0.05),
        "diff_w_out": nrm(ks[17], (N_DIFF, DIFF_HEADS * 2 * DIFF_D, D_MODEL), (DIFF_HEADS * 2 * DIFF_D) ** -0.5),
    }


def reference(x, norm_g, mlp_w1, mlp_w2, gdn_w_in, gdn_conv, gdn_a_log, gdn_dt_bias, gdn_norm_g,
              gdn_w_out, mlstm_w_in, mlstm_gate_b, mlstm_norm_g, mlstm_w_out, diff_w_in,
              diff_lambda, diff_norm_g, diff_w_out):
    t = x.shape[1]
    inv_freq = ROPE_THETA ** (-jnp.arange(0, ROT_DIMS, 2, dtype=jnp.float32) / ROT_DIMS)
    ang = jnp.arange(t, dtype=jnp.float32)[:, None] * inv_freq[None, :]
    cos, sin = jnp.cos(ang).astype(x.dtype), jnp.sin(ang).astype(x.dtype)
    for i in range(DEPTH):
        kind, j = i % N_MIXERS, i // N_MIXERS
        h = rms_norm(x, norm_g[i, 0])
        if kind == 0:
            m = gdn_mixer(h, gdn_w_in[j], gdn_conv[j], gdn_a_log[j], gdn_dt_bias[j], gdn_norm_g[j], gdn_w_out[j])
        elif kind == 1:
            m = mlstm_mixer(h, mlstm_w_in[j], mlstm_gate_b[j], mlstm_norm_g[j], mlstm_w_out[j])
        else:
            lambda_init = 0.8 - 0.6 * math.exp(-0.3 * i)
            m = diff_mixer(h, diff_w_in[j], diff_lambda[j], diff_norm_g[j], diff_w_out[j], lambda_init, cos, sin)
        x = x + rms_norm(m, norm_g[i, 1])
        h = rms_norm(x, norm_g[i, 2])
        x = x + rms_norm(sq_relu_mlp(h, mlp_w1[i], mlp_w2[i]), norm_g[i, 3])
    return x
```

```python
import functools
import math

import jax
import jax.numpy as jnp
from jax import lax
from jax.experimental import pallas as pl
from jax.experimental.pallas import tpu as pltpu

F32 = jnp.float32
BF16 = jnp.bfloat16

D_MODEL = 1024
D_FF = 4 * D_MODEL
HEADS = 8
HEAD_DV = 128
CHUNK = 64
NORM_EPS = 1e-6
GATE_CAP = 15.0
GDN_CONV = 4
ROPE_THETA = 500000.0
DIFF_D = 64
ROT_DIMS = DIFF_D // 4
LANES = 128
HB = 2
GROUPS = HEADS // HB
VMEM_LIMIT = 56 * 1024 * 1024

_NT = (((1,), (1,)), ((), ()))
_TN = (((0,), (0,)), ((), ()))


def _dot(a, b):
    return jnp.dot(a, b, preferred_element_type=F32)


def _dot_nt(a, b):
    return lax.dot_general(a, b, _NT, preferred_element_type=F32)


def _dot_tn(a, b):
    return lax.dot_general(a, b, _TN, preferred_element_type=F32)


def _rms(x, g):
    ms = jnp.mean(x * x, axis=-1, keepdims=True)
    return x * lax.rsqrt(ms + NORM_EPS) * g


def _sigmoid(x):
    return 1.0 / (1.0 + jnp.exp(-x))


def _softplus(x):
    return jnp.maximum(x, 0.0) + jnp.log1p(jnp.exp(-jnp.abs(x)))


def _soft_cap(x):
    return GATE_CAP * jnp.tanh(x / GATE_CAP)


def _split3(x):
    x1 = x.astype(BF16)
    r1 = x - x1.astype(F32)
    x2 = r1.astype(BF16)
    r2 = r1 - x2.astype(F32)
    return x1, x2, r2.astype(BF16)


def _cumsum_rows(x, tril16):
    a, b, c = _split3(x)
    return _dot(tril16, a) + _dot(tril16, b) + _dot(tril16, c)


def _cumsum_lanes(x, triu16):
    a, b, c = _split3(x)
    return _dot(a, triu16) + _dot(b, triu16) + _dot(c, triu16)


def _tri_masks(n):
    r = lax.broadcasted_iota(jnp.int32, (n, n), 0)
    c = lax.broadcasted_iota(jnp.int32, (n, n), 1)
    return r >= c, r > c, r <= c


def _ones_where(mask):
    return jnp.where(mask, 1.0, 0.0).astype(BF16)


def _in_proj_kernel(x_ref, g_ref, w_ref, o_ref, h_ref):
    @pl.when(pl.program_id(1) == 0)
    def _():
        h_ref[...] = _rms(x_ref[...], g_ref[...]).astype(BF16)

    o_ref[...] = _dot(h_ref[...], w_ref[...]).astype(o_ref.dtype)


def _in_proj_gates_kernel(x_ref, g_ref, w_ref, wc_ref, wr_ref, o_ref, oc_ref, or_ref, h_ref):
    @pl.when(pl.program_id(1) == 0)
    def _():
        h = _rms(x_ref[...], g_ref[...]).astype(BF16)
        h_ref[...] = h
        oc_ref[...] = _dot(h, wc_ref[...])
        or_ref[...] = _dot_nt(wr_ref[...], h)

    o_ref[...] = _dot(h_ref[...], w_ref[...]).astype(o_ref.dtype)


def _in_proj(x2, g, w16, wc16=None, wr16=None, *, tm=512, tn=512):
    m, d = x2.shape
    n = w16.shape[1]
    grid = (m // tm, n // tn)
    x_spec = pl.BlockSpec((tm, d), lambda i, j: (i, 0))
    g_spec = pl.BlockSpec((1, d), lambda i, j: (0, 0))
    w_spec = pl.BlockSpec((d, tn), lambda i, j: (0, j))
    o_spec = pl.BlockSpec((tm, tn), lambda i, j: (i, j))
    params = pltpu.CompilerParams(dimension_semantics=("parallel", "arbitrary"),
                                  vmem_limit_bytes=VMEM_LIMIT)
    if wc16 is None:
        return pl.pallas_call(
            _in_proj_kernel, grid=grid,
            in_specs=[x_spec, g_spec, w_spec], out_specs=o_spec,
            out_shape=jax.ShapeDtypeStruct((m, n), F32),
            scratch_shapes=[pltpu.VMEM((tm, d), BF16)],
            compiler_params=params, name="in_proj",
        )(x2, g, w16)
    nc = wc16.shape[1]
    nr = wr16.shape[0]
    return pl.pallas_call(
        _in_proj_gates_kernel, grid=grid,
        in_specs=[x_spec, g_spec, w_spec,
                  pl.BlockSpec((d, nc), lambda i, j: (0, 0)),
                  pl.BlockSpec((nr, d), lambda i, j: (0, 0))],
        out_specs=[o_spec,
                   pl.BlockSpec((tm, nc), lambda i, j: (i, 0)),
                   pl.BlockSpec((nr, tm), lambda i, j: (0, i))],
        out_shape=[jax.ShapeDtypeStruct((m, n), F32),
                   jax.ShapeDtypeStruct((m, nc), F32),
                   jax.ShapeDtypeStruct((nr, m), F32)],
        scratch_shapes=[pltpu.VMEM((tm, d), BF16)],
        compiler_params=params, name="in_proj_gates",
    )(x2, g, w16, wc16, wr16)


def _gate_weights(w_a, w_b):
    d = w_a.shape[0]
    a = w_a.reshape(d, GROUPS, HB)
    b = w_b.reshape(d, GROUPS, HB)
    pad = jnp.zeros((d, GROUPS, LANES - 2 * HB), w_a.dtype)
    wc = jnp.concatenate([a, b, pad], axis=2).reshape(d, GROUPS * LANES)
    wr = jnp.concatenate([w_a, w_b], axis=1).T
    return wc.astype(BF16), wr.astype(BF16)


def _group_rows(p_a, p_b):
    pad = jnp.zeros((GROUPS, LANES - HB), F32)
    ra = jnp.concatenate([p_a.reshape(GROUPS, HB), pad], axis=1)
    rb = jnp.concatenate([p_b.reshape(GROUPS, HB), pad], axis=1)
    return jnp.stack([ra, rb], axis=1)


def _gdn_kernel(q_ref, k_ref, v_ref, z_ref, gc_ref, gr_ref, cwq_ref, cwk_ref, cwv_ref,
                pcol_ref, prow_ref, ng_ref, o_ref,
                xp_ref, qn_ref, kn_ref, vn_ref, bcol_ref, gcol_ref, grow_ref,
                val_ref, kq_ref, att_ref, kst_ref, *, seq):
    nch = seq // CHUNK
    grp = pl.program_id(1)
    tril, strict, triu = _tri_masks(CHUNK)
    tril16 = _ones_where(tril)
    triu16 = _ones_where(triu)

    alog_row = pcol_ref[0, 0:1, :]
    dtb_row = pcol_ref[0, 1:2, :]
    neg_decay_rate = -jnp.exp(alog_row)

    def gate_cols(c, carry):
        rows = pl.ds(pl.multiple_of(c * CHUNK, CHUNK), CHUNK)
        ga = gc_ref[rows, :]
        g = neg_decay_rate * _softplus(ga + dtb_row)
        gcol_ref[rows, :] = _cumsum_rows(g, tril16)
        bcol_ref[rows, :] = _sigmoid(ga)
        return carry

    lax.fori_loop(0, nch, gate_cols, 0)

    for i in range(HB):
        hh = grp * HB + i
        a_r = gr_ref[hh, 0]
        alog = jnp.full((nch, CHUNK), prow_ref[0, hh], F32)
        g_r = -jnp.exp(alog) * _softplus(a_r + prow_ref[1, hh])
        grow_ref[i] = _cumsum_lanes(g_r, triu16)

    tile = 256
    xp_ref[0:8, :] = jnp.zeros((8, LANES), F32)
    jobs = []
    for i in range(HB):
        cs = slice(i * LANES, (i + 1) * LANES)
        jobs.append((q_ref, cwq_ref, qn_ref, cs, HEAD_DV ** -0.5))
        jobs.append((k_ref, cwk_ref, kn_ref, cs, 1.0))
        jobs.append((v_ref, cwv_ref, vn_ref, cs, None))
    for src, cw_ref, dst, cs, scale in jobs:
        xp_ref[8:seq + 8, :] = src[:, cs]
        cw = cw_ref[:, cs]

        def conv_tile(t, carry, dst=dst, cs=cs, cw=cw, scale=scale):
            r0 = pl.multiple_of(t * tile, tile)
            y = cw[3:4, :] * xp_ref[pl.ds(r0 + 8, tile), :]
            for j in range(GDN_CONV - 1):
                y = y + cw[j:j + 1, :] * xp_ref[pl.ds(r0 + 5 + j, tile), :]
            y = y * _sigmoid(y)
            if scale is not None:
                y = y * lax.rsqrt(jnp.sum(y * y, axis=-1, keepdims=True) + NORM_EPS)
                if scale != 1.0:
                    y = y * scale
            dst[pl.ds(r0, tile), cs] = y
            return carry

        lax.fori_loop(0, seq // tile, conv_tile, 0)

    def intra(c, carry):
        r0 = pl.multiple_of(c * CHUNK, CHUNK)
        rows = pl.ds(r0, CHUNK)
        for i in range(HB):
            cs = slice(i * LANES, (i + 1) * LANES)
            q = qn_ref[rows, cs]
            k = kn_ref[rows, cs]
            v = vn_ref[rows, cs]
            beta = bcol_ref[rows, HB + i:HB + i + 1]
            gc = gcol_ref[rows, i:i + 1]
            gr = grow_ref[i, pl.ds(c, 1), :]
            gl = gc[CHUNK - 1:CHUNK, :]
            decay = jnp.where(tril, jnp.exp(jnp.where(tril, gc - gr, 0.0)), 0.0)
            egc = jnp.exp(gc)
            kb = k * beta
            k16 = k.astype(BF16)
            n = jnp.where(strict, -(_dot_nt(kb.astype(BF16), k16) * decay), 0.0)
            inv_m1 = n
            p = n
            for _ in range(5):
                p16 = p.astype(BF16)
                p = _dot(p16, p16)
                inv_m1 = inv_m1 + p + _dot(inv_m1.astype(BF16), p.astype(BF16))
            rhs = jnp.concatenate([v * beta, kb * egc], axis=1)
            sol = rhs + _dot(inv_m1.astype(BF16), rhs.astype(BF16))
            val_ref[rows, cs] = sol[:, :LANES]
            kq_rows = pl.ds(pl.multiple_of(c * (2 * CHUNK), 2 * CHUNK), CHUNK)
            qi_rows = pl.ds(pl.multiple_of(c * (2 * CHUNK) + CHUNK, CHUNK), CHUNK)
            kq_ref[kq_rows, cs] = sol[:, LANES:].astype(BF16)
            kq_ref[qi_rows, cs] = (q * egc).astype(BF16)
            att = jnp.where(tril, _dot_nt(q.astype(BF16), k16) * decay, 0.0)
            att_ref[i, rows, :] = att.astype(BF16)
            kst_ref[rows, cs] = (k * jnp.exp(gl - gc)).astype(BF16)
        return carry

    lax.fori_loop(0, nch, intra, 0)

    ng = ng_ref[...]

    def step(c, states):
        r0 = pl.multiple_of(c * CHUNK, CHUNK)
        rows = pl.ds(r0, CHUNK)
        new_states = []
        for i in range(HB):
            cs = slice(i * LANES, (i + 1) * LANES)
            s = states[i]
            kq = kq_ref[pl.ds(pl.multiple_of(c * (2 * CHUNK), 2 * CHUNK), 2 * CHUNK), cs]
            r = _dot(kq, s.astype(BF16))
            v_new = val_ref[rows, cs] - r[:CHUNK]
            vn16 = v_new.astype(BF16)
            o = r[CHUNK:] + _dot(att_ref[i, rows, :], vn16)
            egl = jnp.exp(gcol_ref[pl.ds(r0 + CHUNK - 1, 1), i:i + 1])
            new_states.append(s * egl + _dot_tn(kst_ref[rows, cs], vn16))
            z = z_ref[rows, cs]
            o_ref[rows, cs] = (_rms(o, ng) * (z * _sigmoid(z))).astype(o_ref.dtype)
        return tuple(new_states)

    init = tuple(jnp.zeros((HEAD_DV, HEAD_DV), F32) for _ in range(HB))
    lax.fori_loop(0, nch, step, init)


def _gdn_core(proj, ga, gat, conv_w, a_log, dt_bias, norm_g, *, batch, seq):
    m = batch * seq
    nch = seq // CHUNK
    wb = HB * LANES
    g_ = GROUPS
    gat4 = gat.reshape(2 * HEADS, batch, nch, CHUNK)
    pcol = _group_rows(a_log, dt_bias)
    prow = jnp.stack([a_log, dt_bias], axis=0)
    kernel = functools.partial(_gdn_kernel, seq=seq)
    return pl.pallas_call(
        kernel, grid=(batch, g_),
        in_specs=[
            pl.BlockSpec((seq, wb), lambda b, g: (b, g)),
            pl.BlockSpec((seq, wb), lambda b, g: (b, g_ + g)),
            pl.BlockSpec((seq, wb), lambda b, g: (b, 2 * g_ + g)),
            pl.BlockSpec((seq, wb), lambda b, g: (b, 3 * g_ + g)),
            pl.BlockSpec((seq, LANES), lambda b, g: (b, g)),
            pl.BlockSpec((2 * HEADS, 1, nch, CHUNK), lambda b, g: (0, b, 0, 0)),
            pl.BlockSpec((GDN_CONV, wb), lambda b, g: (0, g)),
            pl.BlockSpec((GDN_CONV, wb), lambda b, g: (0, g_ + g)),
            pl.BlockSpec((GDN_CONV, wb), lambda b, g: (0, 2 * g_ + g)),
            pl.BlockSpec((1, 2, LANES), lambda b, g: (g, 0, 0)),
            pl.BlockSpec(memory_space=pltpu.SMEM),
            pl.BlockSpec((1, LANES), lambda b, g: (0, 0)),
        ],
        out_specs=pl.BlockSpec((seq, wb), lambda b, g: (b, g)),
        out_shape=jax.ShapeDtypeStruct((m, HEADS * HEAD_DV), BF16),
        scratch_shapes=[
            pltpu.VMEM((seq + 8, LANES), F32),
            pltpu.VMEM((seq, wb), F32),
            pltpu.VMEM((seq, wb), F32),
            pltpu.VMEM((seq, wb), F32),
            pltpu.VMEM((seq, LANES), F32),
            pltpu.VMEM((seq, LANES), F32),
            pltpu.VMEM((HB, nch, CHUNK), F32),
            pltpu.VMEM((seq, wb), F32),
            pltpu.VMEM((2 * seq, wb), BF16),
            pltpu.VMEM((HB, seq, CHUNK), BF16),
            pltpu.VMEM((seq, wb), BF16),
        ],
        compiler_params=pltpu.CompilerParams(dimension_semantics=("parallel", "arbitrary"),
                                             vmem_limit_bytes=VMEM_LIMIT),
        name="gdn_core",
    )(proj, proj, proj, proj, ga, gat4, conv_w, conv_w, conv_w, pcol, prow,
      norm_g.reshape(1, HEAD_DV))


def _mlstm_kernel(q_ref, k_ref, v_ref, og_ref, gc_ref, gr_ref, pcol_ref, prow_ref, ng_ref,
                  o_ref, icol_ref, bcol_ref, drow_ref, c_ref, *, seq):
    nch = seq // CHUNK
    dk = HEAD_DV // 2
    grp = pl.program_id(1)
    tril, _, triu = _tri_masks(CHUNK)
    tril16 = _ones_where(tril)
    triu16 = _ones_where(triu)
    bias_row = pcol_ref[0, 0:1, :]

    def gate_cols(c, carry):
        rows = pl.ds(pl.multiple_of(c * CHUNK, CHUNK), CHUNK)
        pre = _soft_cap(gc_ref[rows, :] + bias_row)
        icol_ref[rows, :] = pre
        bcol_ref[rows, :] = _cumsum_rows(-_softplus(-pre), tril16)
        return carry

    lax.fori_loop(0, nch, gate_cols, 0)

    for i in range(HB):
        hh = grp * HB + i
        i_r = _soft_cap(gr_ref[hh, 0] + prow_ref[0, hh])
        f_r = _soft_cap(gr_ref[HEADS + hh, 0] + prow_ref[1, hh])
        drow_ref[i] = i_r - _cumsum_lanes(-_softplus(-f_r), triu16)

    lane = lax.broadcasted_iota(jnp.int32, (CHUNK, LANES), 1)
    ones_col = _ones_where(lane == 0)
    c_ref[...] = jnp.zeros(c_ref.shape, F32)

    def step(c, m_states):
        r0 = pl.multiple_of(c * CHUNK, CHUNK)
        rows = pl.ds(r0, CHUNK)
        qp = q_ref[rows, :]
        kp = k_ref[rows, :]
        kp16 = kp.astype(BF16)
        new_m = []
        for i in range(HB):
            cs = slice(i * LANES, (i + 1) * LANES)
            head_lanes = (lane >= i * dk) & (lane < (i + 1) * dk)
            m_st = m_states[i]
            bcum = bcol_ref[rows, HB + i:HB + i + 1]
            ig = icol_ref[rows, i:i + 1]
            drow = drow_ref[i, pl.ds(c, 1), :]
            g = bcum[CHUNK - 1:CHUNK, :]
            dlog = jnp.where(tril, bcum + drow, -jnp.inf)
            m_inter = bcum + m_st
            m_out = jnp.maximum(m_inter, jnp.max(dlog, axis=-1, keepdims=True))
            w_inter = jnp.exp(m_inter - m_out)
            q16 = (jnp.where(head_lanes, qp, 0.0) * dk ** -0.5).astype(BF16)
            s = _dot_nt(q16, kp16) * jnp.exp(dlog - m_out)
            vaug = jnp.concatenate([v_ref[rows, cs].astype(BF16), ones_col], axis=1)
            cst = c_ref[i]
            nd = w_inter * _dot(q16, cst.astype(BF16)) + _dot(s.astype(BF16), vaug)
            den = nd[:, LANES:LANES + 1]
            h = nd[:, :LANES] * (1.0 / jnp.maximum(jnp.abs(den), jnp.exp(-m_out)))
            a = g - bcum + ig
            m_new = jnp.maximum(g + m_st, jnp.max(a, axis=0, keepdims=True))
            w_old = jnp.exp(g + m_st - m_new)
            kw = (jnp.where(head_lanes, kp, 0.0) * jnp.exp(a - m_new)).astype(BF16)
            c_ref[i] = w_old * cst + _dot_tn(kw, vaug)
            new_m.append(m_new)
            hn = _rms(h, ng_ref[:, cs])
            o_ref[rows, cs] = (_sigmoid(og_ref[rows, cs]) * hn).astype(o_ref.dtype)
        return tuple(new_m)

    lax.fori_loop(0, nch, step, tuple(jnp.zeros((1, 1), F32) for _ in range(HB)))


def _mlstm_core(proj, ga, gat, gate_b, norm_g, *, batch, seq):
    m = batch * seq
    nch = seq // CHUNK
    wb = HB * LANES
    g_ = GROUPS
    gat4 = gat.reshape(2 * HEADS, batch, nch, CHUNK)
    pad = jnp.zeros((GROUPS, LANES - 2 * HB), F32)
    pcol = jnp.concatenate([gate_b[0].reshape(GROUPS, HB), gate_b[1].reshape(GROUPS, HB), pad],
                           axis=1).reshape(GROUPS, 1, LANES)
    kernel = functools.partial(_mlstm_kernel, seq=seq)
    return pl.pallas_call(
        kernel, grid=(batch, g_),
        in_specs=[
            pl.BlockSpec((seq, LANES), lambda b, g: (b, g)),
            pl.BlockSpec((seq, LANES), lambda b, g: (b, g_ + g)),
            pl.BlockSpec((seq, wb), lambda b, g: (b, g_ + g)),
            pl.BlockSpec((seq, wb), lambda b, g: (b, 2 * g_ + g)),
            pl.BlockSpec((seq, LANES), lambda b, g: (b, g)),
            pl.BlockSpec((2 * HEADS, 1, nch, CHUNK), lambda b, g: (0, b, 0, 0)),
            pl.BlockSpec((1, 1, LANES), lambda b, g: (g, 0, 0)),
            pl.BlockSpec(memory_space=pltpu.SMEM),
            pl.BlockSpec((1, wb), lambda b, g: (0, g)),
        ],
        out_specs=pl.BlockSpec((seq, wb), lambda b, g: (b, g)),
        out_shape=jax.ShapeDtypeStruct((m, HEADS * HEAD_DV), BF16),
        scratch_shapes=[
            pltpu.VMEM((seq, LANES), F32),
            pltpu.VMEM((seq, LANES), F32),
            pltpu.VMEM((HB, nch, CHUNK), F32),
            pltpu.VMEM((HB, LANES, 2 * LANES), F32),
        ],
        compiler_params=pltpu.CompilerParams(dimension_semantics=("parallel", "arbitrary"),
                                             vmem_limit_bytes=VMEM_LIMIT),
        name="mlstm_core",
    )(proj, proj, proj, proj, ga, gat4, pcol, gate_b, norm_g.reshape(1, HEADS * HEAD_DV))


def _rope(x, cos, sin_lo, sin_hi):
    half = ROT_DIMS // 2
    return (x * cos + pltpu.roll(x, LANES - half, axis=1) * sin_lo
            + pltpu.roll(x, half, axis=1) * sin_hi)


def _diff_kernel(q_ref, k_ref, v_ref, tq_ref, tk_ref, lam_ref, ng_ref, o_ref,
                 kr_ref, m_ref, l_ref, acc_ref, *, seq, tq, lambda_init):
    qi = pl.program_id(2)
    tk = tq

    @pl.when(qi == 0)
    def _():
        def rope_tile(t, carry):
            rows = pl.ds(pl.multiple_of(t * tk, tk), tk)
            kr_ref[rows, :] = _rope(k_ref[rows, :], tk_ref[0, rows, :], tk_ref[1, rows, :],
                                    tk_ref[2, rows, :]).astype(BF16)
            return carry
        lax.fori_loop(0, seq // tk, rope_tile, 0)

    q = _rope(q_ref[...], tq_ref[0], tq_ref[1], tq_ref[2]) * DIFF_D ** -0.5
    lane = lax.broadcasted_iota(jnp.int32, (tq, LANES), 1)
    q2 = jnp.concatenate([jnp.where(lane < DIFF_D, q, 0.0), jnp.where(lane >= DIFF_D, q, 0.0)],
                         axis=0).astype(BF16)
    m_ref[...] = jnp.full(m_ref.shape, -jnp.inf, F32)
    l_ref[...] = jnp.zeros(l_ref.shape, F32)
    acc_ref[...] = jnp.zeros(acc_ref.shape, F32)

    def kv_block(j, masked):
        rows = pl.ds(pl.multiple_of(j * tk, tk), tk)
        s = _dot_nt(q2, kr_ref[rows, :])
        if masked:
            r = lax.broadcasted_iota(jnp.int32, (2 * tq, tk), 0)
            c = lax.broadcasted_iota(jnp.int32, (2 * tq, tk), 1)
            r = jnp.where(r >= tq, r - tq, r)
            shift = CHUNK.bit_length() - 1
            s = jnp.where(jnp.right_shift(c, shift) <= jnp.right_shift(r, shift), s, -jnp.inf)
        m_prev = m_ref[...]
        m_new = jnp.maximum(m_prev, jnp.max(s, axis=-1, keepdims=True))
        alpha = jnp.exp(m_prev - m_new)
        p = jnp.exp(s - m_new)
        l_ref[...] = alpha * l_ref[...] + jnp.sum(p, axis=-1, keepdims=True)
        acc_ref[...] = alpha * acc_ref[...] + _dot(p.astype(BF16), v_ref[rows, :].astype(BF16))
        m_ref[...] = m_new

    def body(j, carry):
        kv_block(j, False)
        return carry

    lax.fori_loop(0, qi, body, 0)
    kv_block(qi, True)

    lp = lam_ref[...]
    lam = (jnp.exp(jnp.sum(lp[0:1] * lp[1:2], axis=-1, keepdims=True))
           - jnp.exp(jnp.sum(lp[2:3] * lp[3:4], axis=-1, keepdims=True)) + lambda_init)
    out = acc_ref[...] * (1.0 / l_ref[...])
    o = out[:tq] - lam * out[tq:]
    o_ref[...] = (_rms(o, ng_ref[...]) * (1.0 - lambda_init)).astype(o_ref.dtype)


def _rope_tables(seq):
    half = ROT_DIMS // 2
    inv_freq = ROPE_THETA ** (-jnp.arange(0, ROT_DIMS, 2, dtype=F32) / ROT_DIMS)
    ang = jnp.arange(seq, dtype=F32)[:, None] * inv_freq[None, :]
    cos, sin = jnp.cos(ang), jnp.sin(ang)
    one = jnp.ones((seq, DIFF_D - ROT_DIMS), F32)
    zero = jnp.zeros((seq, DIFF_D - ROT_DIMS), F32)
    zh = jnp.zeros((seq, half), F32)
    cos_c = jnp.concatenate([cos, cos, one], axis=1)
    lo_c = jnp.concatenate([-sin, zh, zero], axis=1)
    hi_c = jnp.concatenate([zh, sin, zero], axis=1)
    rep = lambda t: jnp.concatenate([t, t], axis=1)
    return jnp.stack([rep(cos_c), rep(lo_c), rep(hi_c)], axis=0)


def _diff_core(proj, tables, lam_p, norm_g, lambda_init, *, batch, seq, tq=256):
    m = batch * seq
    nq = seq // tq
    kernel = functools.partial(_diff_kernel, seq=seq, tq=tq, lambda_init=lambda_init)
    return pl.pallas_call(
        kernel, grid=(batch, HEADS, nq),
        in_specs=[
            pl.BlockSpec((tq, LANES), lambda b, h, i: (b * nq + i, h)),
            pl.BlockSpec((seq, LANES), lambda b, h, i: (b, HEADS + h)),
            pl.BlockSpec((seq, LANES), lambda b, h, i: (b, 2 * HEADS + h)),
            pl.BlockSpec((3, tq, LANES), lambda b, h, i: (0, i, 0)),
            pl.BlockSpec((3, seq, LANES), lambda b, h, i: (0, 0, 0)),
            pl.BlockSpec((4, DIFF_D), lambda b, h, i: (0, 0)),
            pl.BlockSpec((1, LANES), lambda b, h, i: (0, 0)),
        ],
        out_specs=pl.BlockSpec((tq, LANES), lambda b, h, i: (b * nq + i, h)),
        out_shape=jax.ShapeDtypeStruct((m, HEADS * 2 * DIFF_D), BF16),
        scratch_shapes=[
            pltpu.VMEM((seq, LANES), BF16),
            pltpu.VMEM((2 * tq, 1), F32),
            pltpu.VMEM((2 * tq, 1), F32),
            pltpu.VMEM((2 * tq, LANES), F32),
        ],
        compiler_params=pltpu.CompilerParams(
            dimension_semantics=("parallel", "parallel", "arbitrary"),
            vmem_limit_bytes=VMEM_LIMIT),
        name="diff_core",
    )(proj, proj, proj, tables, tables, lam_p, norm_g.reshape(1, 2 * DIFF_D))


def _out_proj_kernel(o_ref, w_ref, g_ref, x_ref, y_ref):
    y_ref[...] = x_ref[...] + _rms(_dot(o_ref[...], w_ref[...]), g_ref[...])


def _out_proj(o16, w16, g, x2, *, tm=512):
    m, d = x2.shape
    return pl.pallas_call(
        _out_proj_kernel, grid=(m // tm,),
        in_specs=[pl.BlockSpec((tm, d), lambda i: (i, 0)),
                  pl.BlockSpec((d, d), lambda i: (0, 0)),
                  pl.BlockSpec((1, d), lambda i: (0, 0)),
                  pl.BlockSpec((tm, d), lambda i: (i, 0))],
        out_specs=pl.BlockSpec((tm, d), lambda i: (i, 0)),
        out_shape=jax.ShapeDtypeStruct((m, d), F32),
        compiler_params=pltpu.CompilerParams(dimension_semantics=("parallel",),
                                             vmem_limit_bytes=VMEM_LIMIT),
        name="out_proj",
    )(o16, w16, g, x2)


def _mlp_kernel(x_ref, g2_ref, w1_ref, w2_ref, g3_ref, y_ref, h_ref, acc_ref):
    j = pl.program_id(1)

    @pl.when(j == 0)
    def _():
        h_ref[...] = _rms(x_ref[...], g2_ref[...]).astype(BF16)
        acc_ref[...] = jnp.zeros(acc_ref.shape, F32)

    a = jnp.maximum(_dot(h_ref[...], w1_ref[...]), 0.0)
    acc_ref[...] += _dot((a * a).astype(BF16), w2_ref[...])

    @pl.when(j == pl.num_programs(1) - 1)
    def _():
        y_ref[...] = x_ref[...] + _rms(acc_ref[...], g3_ref[...])


def _mlp(x2, g2, w1_16, w2_16, g3, *, tm=512, tf=512):
    m, d = x2.shape
    f = w1_16.shape[1]
    return pl.pallas_call(
        _mlp_kernel, grid=(m // tm, f // tf),
        in_specs=[pl.BlockSpec((tm, d), lambda i, j: (i, 0)),
                  pl.BlockSpec((1, d), lambda i, j: (0, 0)),
                  pl.BlockSpec((d, tf), lambda i, j: (0, j)),
                  pl.BlockSpec((tf, d), lambda i, j: (j, 0)),
                  pl.BlockSpec((1, d), lambda i, j: (0, 0))],
        out_specs=pl.BlockSpec((tm, d), lambda i, j: (i, 0)),
        out_shape=jax.ShapeDtypeStruct((m, d), F32),
        scratch_shapes=[pltpu.VMEM((tm, d), BF16), pltpu.VMEM((tm, d), F32)],
        compiler_params=pltpu.CompilerParams(dimension_semantics=("parallel", "arbitrary"),
                                             vmem_limit_bytes=VMEM_LIMIT),
        name="mlp",
    )(x2, g2, w1_16, w2_16, g3)


def kernel(x, norm_g, mlp_w1, mlp_w2, gdn_w_in, gdn_conv, gdn_a_log, gdn_dt_bias, gdn_norm_g,
           gdn_w_out, mlstm_w_in, mlstm_gate_b, mlstm_norm_g, mlstm_w_out, diff_w_in,
           diff_lambda, diff_norm_g, diff_w_out):
    batch, seq, d = x.shape
    depth = norm_g.shape[0]
    x2 = x.reshape(batch * seq, d)
    tables = _rope_tables(seq)
    row = lambda v: v.reshape(1, -1)
    for i in range(depth):
        kind, j = i % 3, i // 3
        g_pre = row(norm_g[i, 0])
        if kind == 0:
            w_in = gdn_w_in[j]
            n_main = 4 * HEADS * HEAD_DV
            wc, wr = _gate_weights(w_in[:, n_main:n_main + HEADS], w_in[:, n_main + HEADS:])
            proj, ga, gat = _in_proj(x2, g_pre, w_in[:, :n_main].astype(BF16), wc, wr)
            o16 = _gdn_core(proj, ga, gat, gdn_conv[j], gdn_a_log[j], gdn_dt_bias[j],
                            gdn_norm_g[j], batch=batch, seq=seq)
            w_out = gdn_w_out[j]
        elif kind == 1:
            w_in = mlstm_w_in[j]
            n_main = 3 * HEADS * HEAD_DV
            wc, wr = _gate_weights(w_in[:, n_main:n_main + HEADS], w_in[:, n_main + HEADS:])
            proj, ga, gat = _in_proj(x2, g_pre, w_in[:, :n_main].astype(BF16), wc, wr)
            o16 = _mlstm_core(proj, ga, gat, mlstm_gate_b[j], mlstm_norm_g[j],
                              batch=batch, seq=seq)
            w_out = mlstm_w_out[j]
        else:
            lambda_init = 0.8 - 0.6 * math.exp(-0.3 * i)
            proj = _in_proj(x2, g_pre, diff_w_in[j].astype(BF16))
            o16 = _diff_core(proj, tables, diff_lambda[j], diff_norm_g[j], lambda_init,
                             batch=batch, seq=seq)
            w_out = diff_w_out[j]
        x2 = _out_proj(o16, w_out.astype(BF16), row(norm_g[i, 1]), x2)
        x2 = _mlp(x2, row(norm_g[i, 2]), mlp_w1[i].astype(BF16), mlp_w2[i].astype(BF16),
                  row(norm_g[i, 3]))
    return x2.reshape(batch, seq, d)
```

```python
import functools
import math

import jax
import jax.numpy as jnp
from jax import lax
from jax.experimental import pallas as pl
from jax.experimental.pallas import tpu as pltpu

F32 = jnp.float32
BF16 = jnp.bfloat16

D_MODEL = 1024
D_FF = 4 * D_MODEL
HEADS = 8
HEAD_DV = 128
CHUNK = 64
NORM_EPS = 1e-6
GATE_CAP = 15.0
GDN_CONV = 4
ROPE_THETA = 500000.0
DIFF_D = 64
ROT_DIMS = DIFF_D // 4
LANES = 128
HB = 2
GROUPS = HEADS // HB
VMEM_LIMIT = 56 * 1024 * 1024

_NT = (((1,), (1,)), ((), ()))
_TN = (((0,), (0,)), ((), ()))


def _dot(a, b):
    return jnp.dot(a, b, preferred_element_type=F32)


def _dot_nt(a, b):
    return lax.dot_general(a, b, _NT, preferred_element_type=F32)


def _dot_tn(a, b):
    return lax.dot_general(a, b, _TN, preferred_element_type=F32)


def _rms(x, g):
    ms = jnp.mean(x * x, axis=-1, keepdims=True)
    return x * lax.rsqrt(ms + NORM_EPS) * g


def _sigmoid(x):
    return 1.0 / (1.0 + jnp.exp(-x))


def _softplus(x):
    return jnp.maximum(x, 0.0) + jnp.log1p(jnp.exp(-jnp.abs(x)))


def _soft_cap(x):
    return GATE_CAP * jnp.tanh(x / GATE_CAP)


def _split3(x):
    x1 = x.astype(BF16)
    r1 = x - x1.astype(F32)
    x2 = r1.astype(BF16)
    r2 = r1 - x2.astype(F32)
    return x1, x2, r2.astype(BF16)


def _cumsum_rows(x, tril16):
    a, b, c = _split3(x)
    return _dot(tril16, a) + _dot(tril16, b) + _dot(tril16, c)


def _cumsum_lanes(x, triu16):
    a, b, c = _split3(x)
    return _dot(a, triu16) + _dot(b, triu16) + _dot(c, triu16)


def _split2(x):
    hi = x.astype(BF16)
    return hi, (x - hi.astype(F32)).astype(BF16)


def _dot3(a, b):
    ah, al = _split2(a)
    bh, bl = _split2(b)
    return _dot(ah, bh) + _dot(ah, bl) + _dot(al, bh)


def _unit_lower_inverse_minus_identity(a):
    n = a.shape[0]
    r = lax.broadcasted_iota(jnp.int32, (n, n), 0)
    c = lax.broadcasted_iota(jnp.int32, (n, n), 1)
    inv_m1 = -jnp.where(jnp.right_shift(r, 1) == jnp.right_shift(c, 1), a, 0.0)
    s = 2
    while s < n:
        sh = s.bit_length() - 1
        merge = ((jnp.right_shift(r, sh + 1) == jnp.right_shift(c, sh + 1))
                 & (jnp.right_shift(r, sh) != jnp.right_shift(c, sh)))
        off = jnp.where(merge, a, 0.0)
        y = off + _dot3(off, inv_m1)
        inv_m1 = inv_m1 - (y + _dot3(inv_m1, y))
        s *= 2
    return inv_m1


def _tri_masks(n):
    r = lax.broadcasted_iota(jnp.int32, (n, n), 0)
    c = lax.broadcasted_iota(jnp.int32, (n, n), 1)
    return r >= c, r > c, r <= c


def _ones_where(mask):
    return jnp.where(mask, 1.0, 0.0).astype(BF16)


def _in_proj_kernel(x_ref, g_ref, w_ref, o_ref, h_ref):
    @pl.when(pl.program_id(1) == 0)
    def _():
        h_ref[...] = _rms(x_ref[...], g_ref[...]).astype(BF16)

    o_ref[...] = _dot(h_ref[...], w_ref[...]).astype(o_ref.dtype)


def _in_proj_gates_kernel(x_ref, g_ref, w_ref, wc_ref, wr_ref, o_ref, oc_ref, or_ref, h_ref):
    @pl.when(pl.program_id(1) == 0)
    def _():
        h = _rms(x_ref[...], g_ref[...]).astype(BF16)
        h_ref[...] = h
        oc_ref[...] = _dot(h, wc_ref[...])
        or_ref[...] = _dot_nt(wr_ref[...], h)

    o_ref[...] = _dot(h_ref[...], w_ref[...]).astype(o_ref.dtype)


def _in_proj(x2, g, w16, wc16=None, wr16=None, *, tm=512, tn=512):
    m, d = x2.shape
    n = w16.shape[1]
    grid = (m // tm, n // tn)
    x_spec = pl.BlockSpec((tm, d), lambda i, j: (i, 0))
    g_spec = pl.BlockSpec((1, d), lambda i, j: (0, 0))
    w_spec = pl.BlockSpec((d, tn), lambda i, j: (0, j))
    o_spec = pl.BlockSpec((tm, tn), lambda i, j: (i, j))
    params = pltpu.CompilerParams(dimension_semantics=("parallel", "arbitrary"),
                                  vmem_limit_bytes=VMEM_LIMIT)
    if wc16 is None:
        return pl.pallas_call(
            _in_proj_kernel, grid=grid,
            in_specs=[x_spec, g_spec, w_spec], out_specs=o_spec,
            out_shape=jax.ShapeDtypeStruct((m, n), F32),
            scratch_shapes=[pltpu.VMEM((tm, d), BF16)],
            compiler_params=params, name="in_proj",
        )(x2, g, w16)
    nc = wc16.shape[1]
    nr = wr16.shape[0]
    return pl.pallas_call(
        _in_proj_gates_kernel, grid=grid,
        in_specs=[x_spec, g_spec, w_spec,
                  pl.BlockSpec((d, nc), lambda i, j: (0, 0)),
                  pl.BlockSpec((nr, d), lambda i, j: (0, 0))],
        out_specs=[o_spec,
                   pl.BlockSpec((tm, nc), lambda i, j: (i, 0)),
                   pl.BlockSpec((nr, tm), lambda i, j: (0, i))],
        out_shape=[jax.ShapeDtypeStruct((m, n), F32),
                   jax.ShapeDtypeStruct((m, nc), F32),
                   jax.ShapeDtypeStruct((nr, m), F32)],
        scratch_shapes=[pltpu.VMEM((tm, d), BF16)],
        compiler_params=params, name="in_proj_gates",
    )(x2, g, w16, wc16, wr16)


def _gate_weights(w_a, w_b):
    d = w_a.shape[0]
    a = w_a.reshape(d, GROUPS, HB)
    b = w_b.reshape(d, GROUPS, HB)
    pad = jnp.zeros((d, GROUPS, LANES - 2 * HB), w_a.dtype)
    wc = jnp.concatenate([a, b, pad], axis=2).reshape(d, GROUPS * LANES)
    wr = jnp.concatenate([w_a, w_b], axis=1).T
    return wc.astype(BF16), wr.astype(BF16)


def _group_rows(p_a, p_b):
    pad = jnp.zeros((GROUPS, LANES - HB), F32)
    ra = jnp.concatenate([p_a.reshape(GROUPS, HB), pad], axis=1)
    rb = jnp.concatenate([p_b.reshape(GROUPS, HB), pad], axis=1)
    return jnp.stack([ra, rb], axis=1)


def _gdn_kernel(q_ref, k_ref, v_ref, z_ref, gc_ref, gr_ref, cwq_ref, cwk_ref, cwv_ref,
                pcol_ref, prow_ref, ng_ref, o_ref,
                xp_ref, qn_ref, kn_ref, vn_ref, bcol_ref, gcol_ref, grow_ref,
                val_ref, kq_ref, att_ref, kst_ref, *, seq):
    nch = seq // CHUNK
    grp = pl.program_id(1)
    tril, strict, triu = _tri_masks(CHUNK)
    tril16 = _ones_where(tril)
    triu16 = _ones_where(triu)

    alog_row = pcol_ref[0, 0:1, :]
    dtb_row = pcol_ref[0, 1:2, :]
    neg_decay_rate = -jnp.exp(alog_row)

    def gate_cols(c, carry):
        rows = pl.ds(pl.multiple_of(c * CHUNK, CHUNK), CHUNK)
        ga = gc_ref[rows, :]
        g = neg_decay_rate * _softplus(ga + dtb_row)
        gcol_ref[rows, :] = _cumsum_rows(g, tril16)
        bcol_ref[rows, :] = _sigmoid(ga)
        return carry

    lax.fori_loop(0, nch, gate_cols, 0)

    for i in range(HB):
        hh = grp * HB + i
        a_r = gr_ref[hh, 0]
        alog = jnp.full((nch, CHUNK), prow_ref[0, hh], F32)
        g_r = -jnp.exp(alog) * _softplus(a_r + prow_ref[1, hh])
        grow_ref[i] = _cumsum_lanes(g_r, triu16)

    tile = 256
    xp_ref[0:8, :] = jnp.zeros((8, LANES), F32)
    jobs = []
    for i in range(HB):
        cs = slice(i * LANES, (i + 1) * LANES)
        jobs.append((q_ref, cwq_ref, qn_ref, cs, HEAD_DV ** -0.5))
        jobs.append((k_ref, cwk_ref, kn_ref, cs, 1.0))
        jobs.append((v_ref, cwv_ref, vn_ref, cs, None))
    for src, cw_ref, dst, cs, scale in jobs:
        xp_ref[8:seq + 8, :] = src[:, cs]
        cw = cw_ref[:, cs]

        def conv_tile(t, carry, dst=dst, cs=cs, cw=cw, scale=scale):
            r0 = pl.multiple_of(t * tile, tile)
            y = cw[3:4, :] * xp_ref[pl.ds(r0 + 8, tile), :]
            for j in range(GDN_CONV - 1):
                y = y + cw[j:j + 1, :] * xp_ref[pl.ds(r0 + 5 + j, tile), :]
            y = y * _sigmoid(y)
            if scale is not None:
                y = y * lax.rsqrt(jnp.sum(y * y, axis=-1, keepdims=True) + NORM_EPS)
                if scale != 1.0:
                    y = y * scale
            dst[pl.ds(r0, tile), cs] = y
            return carry

        lax.fori_loop(0, seq // tile, conv_tile, 0)

    def intra(c, carry):
        r0 = pl.multiple_of(c * CHUNK, CHUNK)
        rows = pl.ds(r0, CHUNK)
        for i in range(HB):
            cs = slice(i * LANES, (i + 1) * LANES)
            q = qn_ref[rows, cs]
            k = kn_ref[rows, cs]
            v = vn_ref[rows, cs]
            beta = bcol_ref[rows, HB + i:HB + i + 1]
            gc = gcol_ref[rows, i:i + 1]
            gr = grow_ref[i, pl.ds(c, 1), :]
            gl = gc[CHUNK - 1:CHUNK, :]
            decay = jnp.where(tril, jnp.exp(jnp.where(tril, gc - gr, 0.0)), 0.0)
            egc = jnp.exp(gc)
            kb = k * beta
            k16 = k.astype(BF16)
            a_mat = jnp.where(strict, _dot_nt(kb.astype(BF16), k16) * decay, 0.0)
            inv_m1 = _unit_lower_inverse_minus_identity(a_mat)
            rhs = jnp.concatenate([v * beta, kb * egc], axis=1)
            sol = rhs + _dot3(inv_m1, rhs)
            val_ref[rows, cs] = sol[:, :LANES]
            kq_rows = pl.ds(pl.multiple_of(c * (2 * CHUNK), 2 * CHUNK), CHUNK)
            qi_rows = pl.ds(pl.multiple_of(c * (2 * CHUNK) + CHUNK, CHUNK), CHUNK)
            kq_ref[kq_rows, cs] = sol[:, LANES:].astype(BF16)
            kq_ref[qi_rows, cs] = (q * egc).astype(BF16)
            att = jnp.where(tril, _dot_nt(q.astype(BF16), k16) * decay, 0.0)
            att_ref[i, rows, :] = att.astype(BF16)
            kst_ref[rows, cs] = (k * jnp.exp(gl - gc)).astype(BF16)
        return carry

    lax.fori_loop(0, nch, intra, 0)

    ng = ng_ref[...]

    def step(c, states):
        r0 = pl.multiple_of(c * CHUNK, CHUNK)
        rows = pl.ds(r0, CHUNK)
        new_states = []
        for i in range(HB):
            cs = slice(i * LANES, (i + 1) * LANES)
            s = states[i]
            kq = kq_ref[pl.ds(pl.multiple_of(c * (2 * CHUNK), 2 * CHUNK), 2 * CHUNK), cs]
            r = _dot(kq, s.astype(BF16))
            v_new = val_ref[rows, cs] - r[:CHUNK]
            vn16 = v_new.astype(BF16)
            o = r[CHUNK:] + _dot(att_ref[i, rows, :], vn16)
            egl = jnp.exp(gcol_ref[pl.ds(r0 + CHUNK - 1, 1), i:i + 1])
            new_states.append(s * egl + _dot_tn(kst_ref[rows, cs], vn16))
            z = z_ref[rows, cs]
            o_ref[rows, cs] = (_rms(o, ng) * (z * _sigmoid(z))).astype(o_ref.dtype)
        return tuple(new_states)

    init = tuple(jnp.zeros((HEAD_DV, HEAD_DV), F32) for _ in range(HB))
    lax.fori_loop(0, nch, step, init)


def _gdn_core(proj, ga, gat, conv_w, a_log, dt_bias, norm_g, *, batch, seq):
    m = batch * seq
    nch = seq // CHUNK
    wb = HB * LANES
    g_ = GROUPS
    gat4 = gat.reshape(2 * HEADS, batch, nch, CHUNK)
    pcol = _group_rows(a_log, dt_bias)
    prow = jnp.stack([a_log, dt_bias], axis=0)
    kernel = functools.partial(_gdn_kernel, seq=seq)
    return pl.pallas_call(
        kernel, grid=(batch, g_),
        in_specs=[
            pl.BlockSpec((seq, wb), lambda b, g: (b, g)),
            pl.BlockSpec((seq, wb), lambda b, g: (b, g_ + g)),
            pl.BlockSpec((seq, wb), lambda b, g: (b, 2 * g_ + g)),
            pl.BlockSpec((seq, wb), lambda b, g: (b, 3 * g_ + g)),
            pl.BlockSpec((seq, LANES), lambda b, g: (b, g)),
            pl.BlockSpec((2 * HEADS, 1, nch, CHUNK), lambda b, g: (0, b, 0, 0)),
            pl.BlockSpec((GDN_CONV, wb), lambda b, g: (0, g)),
            pl.BlockSpec((GDN_CONV, wb), lambda b, g: (0, g_ + g)),
            pl.BlockSpec((GDN_CONV, wb), lambda b, g: (0, 2 * g_ + g)),
            pl.BlockSpec((1, 2, LANES), lambda b, g: (g, 0, 0)),
            pl.BlockSpec(memory_space=pltpu.SMEM),
            pl.BlockSpec((1, LANES), lambda b, g: (0, 0)),
        ],
        out_specs=pl.BlockSpec((seq, wb), lambda b, g: (b, g)),
        out_shape=jax.ShapeDtypeStruct((m, HEADS * HEAD_DV), BF16),
        scratch_shapes=[
            pltpu.VMEM((seq + 8, LANES), F32),
            pltpu.VMEM((seq, wb), F32),
            pltpu.VMEM((seq, wb), F32),
            pltpu.VMEM((seq, wb), F32),
            pltpu.VMEM((seq, LANES), F32),
            pltpu.VMEM((seq, LANES), F32),
            pltpu.VMEM((HB, nch, CHUNK), F32),
            pltpu.VMEM((seq, wb), F32),
            pltpu.VMEM((2 * seq, wb), BF16),
            pltpu.VMEM((HB, seq, CHUNK), BF16),
            pltpu.VMEM((seq, wb), BF16),
        ],
        compiler_params=pltpu.CompilerParams(dimension_semantics=("parallel", "arbitrary"),
                                             vmem_limit_bytes=VMEM_LIMIT),
        name="gdn_core",
    )(proj, proj, proj, proj, ga, gat4, conv_w, conv_w, conv_w, pcol, prow,
      norm_g.reshape(1, HEAD_DV))


def _mlstm_kernel(q_ref, k_ref, v_ref, og_ref, gc_ref, gr_ref, pcol_ref, prow_ref, ng_ref,
                  o_ref, icol_ref, bcol_ref, drow_ref, c_ref, *, seq):
    nch = seq // CHUNK
    dk = HEAD_DV // 2
    grp = pl.program_id(1)
    tril, _, triu = _tri_masks(CHUNK)
    tril16 = _ones_where(tril)
    triu16 = _ones_where(triu)
    bias_row = pcol_ref[0, 0:1, :]

    def gate_cols(c, carry):
        rows = pl.ds(pl.multiple_of(c * CHUNK, CHUNK), CHUNK)
        pre = _soft_cap(gc_ref[rows, :] + bias_row)
        icol_ref[rows, :] = pre
        bcol_ref[rows, :] = _cumsum_rows(-_softplus(-pre), tril16)
        return carry

    lax.fori_loop(0, nch, gate_cols, 0)

    for i in range(HB):
        hh = grp * HB + i
        i_r = _soft_cap(gr_ref[hh, 0] + prow_ref[0, hh])
        f_r = _soft_cap(gr_ref[HEADS + hh, 0] + prow_ref[1, hh])
        drow_ref[i] = i_r - _cumsum_lanes(-_softplus(-f_r), triu16)

    lane = lax.broadcasted_iota(jnp.int32, (CHUNK, LANES), 1)
    ones_col = _ones_where(lane == 0)
    c_ref[...] = jnp.zeros(c_ref.shape, F32)

    def step(c, m_states):
        r0 = pl.multiple_of(c * CHUNK, CHUNK)
        rows = pl.ds(r0, CHUNK)
        qp = q_ref[rows, :]
        kp = k_ref[rows, :]
        kp16 = kp.astype(BF16)
        new_m = []
        for i in range(HB):
            cs = slice(i * LANES, (i + 1) * LANES)
            head_lanes = (lane >= i * dk) & (lane < (i + 1) * dk)
            m_st = m_states[i]
            bcum = bcol_ref[rows, HB + i:HB + i + 1]
            ig = icol_ref[rows, i:i + 1]
            drow = drow_ref[i, pl.ds(c, 1), :]
            g = bcum[CHUNK - 1:CHUNK, :]
            dlog = jnp.where(tril, bcum + drow, -jnp.inf)
            m_inter = bcum + m_st
            m_out = jnp.maximum(m_inter, jnp.max(dlog, axis=-1, keepdims=True))
            w_inter = jnp.exp(m_inter - m_out)
            q16 = (jnp.where(head_lanes, qp, 0.0) * dk ** -0.5).astype(BF16)
            s = _dot_nt(q16, kp16) * jnp.exp(dlog - m_out)
            vaug = jnp.concatenate([v_ref[rows, cs].astype(BF16), ones_col], axis=1)
            cst = c_ref[i]
            nd = w_inter * _dot(q16, cst.astype(BF16)) + _dot(s.astype(BF16), vaug)
            den = nd[:, LANES:LANES + 1]
            h = nd[:, :LANES] * (1.0 / jnp.maximum(jnp.abs(den), jnp.exp(-m_out)))
            a = g - bcum + ig
            m_new = jnp.maximum(g + m_st, jnp.max(a, axis=0, keepdims=True))
            w_old = jnp.exp(g + m_st - m_new)
            kw = (jnp.where(head_lanes, kp, 0.0) * jnp.exp(a - m_new)).astype(BF16)
            c_ref[i] = w_old * cst + _dot_tn(kw, vaug)
            new_m.append(m_new)
            hn = _rms(h, ng_ref[:, cs])
            o_ref[rows, cs] = (_sigmoid(og_ref[rows, cs]) * hn).astype(o_ref.dtype)
        return tuple(new_m)

    lax.fori_loop(0, nch, step, tuple(jnp.zeros((1, 1), F32) for _ in range(HB)))


def _mlstm_core(proj, ga, gat, gate_b, norm_g, *, batch, seq):
    m = batch * seq
    nch = seq // CHUNK
    wb = HB * LANES
    g_ = GROUPS
    gat4 = gat.reshape(2 * HEADS, batch, nch, CHUNK)
    pad = jnp.zeros((GROUPS, LANES - 2 * HB), F32)
    pcol = jnp.concatenate([gate_b[0].reshape(GROUPS, HB), gate_b[1].reshape(GROUPS, HB), pad],
                           axis=1).reshape(GROUPS, 1, LANES)
    kernel = functools.partial(_mlstm_kernel, seq=seq)
    return pl.pallas_call(
        kernel, grid=(batch, g_),
        in_specs=[
            pl.BlockSpec((seq, LANES), lambda b, g: (b, g)),
            pl.BlockSpec((seq, LANES), lambda b, g: (b, g_ + g)),
            pl.BlockSpec((seq, wb), lambda b, g: (b, g_ + g)),
            pl.BlockSpec((seq, wb), lambda b, g: (b, 2 * g_ + g)),
            pl.BlockSpec((seq, LANES), lambda b, g: (b, g)),
            pl.BlockSpec((2 * HEADS, 1, nch, CHUNK), lambda b, g: (0, b, 0, 0)),
            pl.BlockSpec((1, 1, LANES), lambda b, g: (g, 0, 0)),
            pl.BlockSpec(memory_space=pltpu.SMEM),
            pl.BlockSpec((1, wb), lambda b, g: (0, g)),
        ],
        out_specs=pl.BlockSpec((seq, wb), lambda b, g: (b, g)),
        out_shape=jax.ShapeDtypeStruct((m, HEADS * HEAD_DV), BF16),
        scratch_shapes=[
            pltpu.VMEM((seq, LANES), F32),
            pltpu.VMEM((seq, LANES), F32),
            pltpu.VMEM((HB, nch, CHUNK), F32),
            pltpu.VMEM((HB, LANES, 2 * LANES), F32),
        ],
        compiler_params=pltpu.CompilerParams(dimension_semantics=("parallel", "arbitrary"),
                                             vmem_limit_bytes=VMEM_LIMIT),
        name="mlstm_core",
    )(proj, proj, proj, proj, ga, gat4, pcol, gate_b, norm_g.reshape(1, HEADS * HEAD_DV))


def _rope(x, cos, sin_lo, sin_hi):
    half = ROT_DIMS // 2
    return (x * cos + pltpu.roll(x, LANES - half, axis=1) * sin_lo
            + pltpu.roll(x, half, axis=1) * sin_hi)


def _diff_kernel(q_ref, k_ref, v_ref, tq_ref, tk_ref, lam_ref, ng_ref, o_ref,
                 kr_ref, vt_ref, q2_ref, m_ref, l_ref, acc_ref, *, seq, tq, dh, lambda_init):
    qi = pl.program_id(2)
    tk = tq

    @pl.when(qi == 0)
    def _():
        def prep_tile(t, carry):
            rows = pl.ds(pl.multiple_of(t * tk, tk), tk)
            for d in range(dh):
                cs = slice(d * LANES, (d + 1) * LANES)
                kr_ref[d, rows, :] = _rope(k_ref[rows, cs], tk_ref[0, rows, :], tk_ref[1, rows, :],
                                           tk_ref[2, rows, :]).astype(BF16)
                vt_ref[d, :, rows] = v_ref[rows, cs].T.astype(BF16)
            return carry
        lax.fori_loop(0, seq // tk, prep_tile, 0)

    lane = lax.broadcasted_iota(jnp.int32, (tq, LANES), 1)
    for d in range(dh):
        cs = slice(d * LANES, (d + 1) * LANES)
        q = _rope(q_ref[:, cs], tq_ref[0], tq_ref[1], tq_ref[2]) * DIFF_D ** -0.5
        q2_ref[d] = jnp.concatenate(
            [jnp.where(lane < DIFF_D, q, 0.0), jnp.where(lane >= DIFF_D, q, 0.0)],
            axis=0).astype(BF16)
    m_ref[...] = jnp.full(m_ref.shape, -jnp.inf, F32)
    l_ref[...] = jnp.zeros(l_ref.shape, F32)
    acc_ref[...] = jnp.zeros(acc_ref.shape, F32)

    def kv_block(d, j, masked):
        rows = pl.ds(pl.multiple_of(j * tk, tk), tk)
        st = _dot_nt(kr_ref[d, rows, :], q2_ref[d])
        if masked:
            kpos = lax.broadcasted_iota(jnp.int32, (tk, 2 * tq), 0)
            qpos = lax.broadcasted_iota(jnp.int32, (tk, 2 * tq), 1)
            qpos = jnp.where(qpos >= tq, qpos - tq, qpos)
            shift = CHUNK.bit_length() - 1
            st = jnp.where(jnp.right_shift(kpos, shift) <= jnp.right_shift(qpos, shift),
                           st, -jnp.inf)
        m_prev = m_ref[d]
        m_new = jnp.maximum(m_prev, jnp.max(st, axis=0, keepdims=True))
        alpha = jnp.exp(m_prev - m_new)
        p = jnp.exp(st - m_new)
        l_ref[d] = alpha * l_ref[d] + jnp.sum(p, axis=0, keepdims=True)
        acc_ref[d] = alpha * acc_ref[d] + _dot(vt_ref[d, :, rows], p.astype(BF16))
        m_ref[d] = m_new

    def body(j, carry):
        for d in range(dh):
            kv_block(d, j, False)
        return carry

    lax.fori_loop(0, qi, body, 0)

    lp = lam_ref[...]
    lam = (jnp.exp(jnp.sum(lp[0:1] * lp[1:2], axis=-1, keepdims=True))
           - jnp.exp(jnp.sum(lp[2:3] * lp[3:4], axis=-1, keepdims=True)) + lambda_init)
    for d in range(dh):
        kv_block(d, qi, True)
        out_t = acc_ref[d] * (1.0 / l_ref[d])
        o = (out_t[:, :tq] - lam * out_t[:, tq:]).T
        o_ref[:, d * LANES:(d + 1) * LANES] = (
            _rms(o, ng_ref[...]) * (1.0 - lambda_init)).astype(o_ref.dtype)


def _rope_tables(seq):
    half = ROT_DIMS // 2
    inv_freq = ROPE_THETA ** (-jnp.arange(0, ROT_DIMS, 2, dtype=F32) / ROT_DIMS)
    ang = jnp.arange(seq, dtype=F32)[:, None] * inv_freq[None, :]
    cos, sin = jnp.cos(ang), jnp.sin(ang)
    one = jnp.ones((seq, DIFF_D - ROT_DIMS), F32)
    zero = jnp.zeros((seq, DIFF_D - ROT_DIMS), F32)
    zh = jnp.zeros((seq, half), F32)
    cos_c = jnp.concatenate([cos, cos, one], axis=1)
    lo_c = jnp.concatenate([-sin, zh, zero], axis=1)
    hi_c = jnp.concatenate([zh, sin, zero], axis=1)
    rep = lambda t: jnp.concatenate([t, t], axis=1)
    return jnp.stack([rep(cos_c), rep(lo_c), rep(hi_c)], axis=0)


def _diff_core(proj, tables, lam_p, norm_g, lambda_init, *, batch, seq, tq=256, dh=2):
    m = batch * seq
    nq = seq // tq
    hg = HEADS // dh
    wb = dh * LANES
    kernel = functools.partial(_diff_kernel, seq=seq, tq=tq, dh=dh, lambda_init=lambda_init)
    return pl.pallas_call(
        kernel, grid=(batch, hg, nq),
        in_specs=[
            pl.BlockSpec((tq, wb), lambda b, h, i: (b * nq + i, h)),
            pl.BlockSpec((seq, wb), lambda b, h, i: (b, hg + h)),
            pl.BlockSpec((seq, wb), lambda b, h, i: (b, 2 * hg + h)),
            pl.BlockSpec((3, tq, LANES), lambda b, h, i: (0, i, 0)),
            pl.BlockSpec((3, seq, LANES), lambda b, h, i: (0, 0, 0)),
            pl.BlockSpec((4, DIFF_D), lambda b, h, i: (0, 0)),
            pl.BlockSpec((1, LANES), lambda b, h, i: (0, 0)),
        ],
        out_specs=pl.BlockSpec((tq, wb), lambda b, h, i: (b * nq + i, h)),
        out_shape=jax.ShapeDtypeStruct((m, HEADS * 2 * DIFF_D), BF16),
        scratch_shapes=[
            pltpu.VMEM((dh, seq, LANES), BF16),
            pltpu.VMEM((dh, LANES, seq), BF16),
            pltpu.VMEM((dh, 2 * tq, LANES), BF16),
            pltpu.VMEM((dh, 1, 2 * tq), F32),
            pltpu.VMEM((dh, 1, 2 * tq), F32),
            pltpu.VMEM((dh, LANES, 2 * tq), F32),
        ],
        compiler_params=pltpu.CompilerParams(
            dimension_semantics=("parallel", "parallel", "arbitrary"),
            vmem_limit_bytes=VMEM_LIMIT),
        name="diff_core",
    )(proj, proj, proj, tables, tables, lam_p, norm_g.reshape(1, 2 * DIFF_D))


def _out_proj_kernel(o_ref, w_ref, g_ref, x_ref, y_ref):
    y_ref[...] = x_ref[...] + _rms(_dot(o_ref[...], w_ref[...]), g_ref[...])


def _out_proj(o16, w16, g, x2, *, tm=512):
    m, d = x2.shape
    return pl.pallas_call(
        _out_proj_kernel, grid=(m // tm,),
        in_specs=[pl.BlockSpec((tm, d), lambda i: (i, 0)),
                  pl.BlockSpec((d, d), lambda i: (0, 0)),
                  pl.BlockSpec((1, d), lambda i: (0, 0)),
                  pl.BlockSpec((tm, d), lambda i: (i, 0))],
        out_specs=pl.BlockSpec((tm, d), lambda i: (i, 0)),
        out_shape=jax.ShapeDtypeStruct((m, d), F32),
        compiler_params=pltpu.CompilerParams(dimension_semantics=("parallel",),
                                             vmem_limit_bytes=VMEM_LIMIT),
        name="out_proj",
    )(o16, w16, g, x2)


def _mlp_kernel(x_ref, g2_ref, w1_ref, w2_ref, g3_ref, y_ref, h_ref, acc_ref):
    j = pl.program_id(1)

    @pl.when(j == 0)
    def _():
        h_ref[...] = _rms(x_ref[...], g2_ref[...]).astype(BF16)
        acc_ref[...] = jnp.zeros(acc_ref.shape, F32)

    a = jnp.maximum(_dot(h_ref[...], w1_ref[...]), 0.0)
    acc_ref[...] += _dot((a * a).astype(BF16), w2_ref[...])

    @pl.when(j == pl.num_programs(1) - 1)
    def _():
        y_ref[...] = x_ref[...] + _rms(acc_ref[...], g3_ref[...])


def _mlp(x2, g2, w1_16, w2_16, g3, *, tm=512, tf=512):
    m, d = x2.shape
    f = w1_16.shape[1]
    return pl.pallas_call(
        _mlp_kernel, grid=(m // tm, f // tf),
        in_specs=[pl.BlockSpec((tm, d), lambda i, j: (i, 0)),
                  pl.BlockSpec((1, d), lambda i, j: (0, 0)),
                  pl.BlockSpec((d, tf), lambda i, j: (0, j)),
                  pl.BlockSpec((tf, d), lambda i, j: (j, 0)),
                  pl.BlockSpec((1, d), lambda i, j: (0, 0))],
        out_specs=pl.BlockSpec((tm, d), lambda i, j: (i, 0)),
        out_shape=jax.ShapeDtypeStruct((m, d), F32),
        scratch_shapes=[pltpu.VMEM((tm, d), BF16), pltpu.VMEM((tm, d), F32)],
        compiler_params=pltpu.CompilerParams(dimension_semantics=("parallel", "arbitrary"),
                                             vmem_limit_bytes=VMEM_LIMIT),
        name="mlp",
    )(x2, g2, w1_16, w2_16, g3)


def kernel(x, norm_g, mlp_w1, mlp_w2, gdn_w_in, gdn_conv, gdn_a_log, gdn_dt_bias, gdn_norm_g,
           gdn_w_out, mlstm_w_in, mlstm_gate_b, mlstm_norm_g, mlstm_w_out, diff_w_in,
           diff_lambda, diff_norm_g, diff_w_out):
    batch, seq, d = x.shape
    depth = norm_g.shape[0]
    x2 = x.reshape(batch * seq, d)
    tables = _rope_tables(seq)
    row = lambda v: v.reshape(1, -1)
    for i in range(depth):
        kind, j = i % 3, i // 3
        g_pre = row(norm_g[i, 0])
        if kind == 0:
            w_in = gdn_w_in[j]
            n_main = 4 * HEADS * HEAD_DV
            wc, wr = _gate_weights(w_in[:, n_main:n_main + HEADS], w_in[:, n_main + HEADS:])
            proj, ga, gat = _in_proj(x2, g_pre, w_in[:, :n_main].astype(BF16), wc, wr)
            o16 = _gdn_core(proj, ga, gat, gdn_conv[j], gdn_a_log[j], gdn_dt_bias[j],
                            gdn_norm_g[j], batch=batch, seq=seq)
            w_out = gdn_w_out[j]
        elif kind == 1:
            w_in = mlstm_w_in[j]
            n_main = 3 * HEADS * HEAD_DV
            wc, wr = _gate_weights(w_in[:, n_main:n_main + HEADS], w_in[:, n_main + HEADS:])
            proj, ga, gat = _in_proj(x2, g_pre, w_in[:, :n_main].astype(BF16), wc, wr)
            o16 = _mlstm_core(proj, ga, gat, mlstm_gate_b[j], mlstm_norm_g[j],
                              batch=batch, seq=seq)
            w_out = mlstm_w_out[j]
        else:
            lambda_init = 0.8 - 0.6 * math.exp(-0.3 * i)
            proj = _in_proj(x2, g_pre, diff_w_in[j].astype(BF16))
            o16 = _diff_core(proj, tables, diff_lambda[j], diff_norm_g[j], lambda_init,
                             batch=batch, seq=seq)
            w_out = diff_w_out[j]
        x2 = _out_proj(o16, w_out.astype(BF16), row(norm_g[i, 1]), x2)
        x2 = _mlp(x2, row(norm_g[i, 2]), mlp_w1[i].astype(BF16), mlp_w2[i].astype(BF16),
                  row(norm_g[i, 3]))
    return x2.reshape(batch, seq, d)
```

```python
import functools
import math

import jax
import jax.numpy as jnp
from jax import lax
from jax.experimental import pallas as pl
from jax.experimental.pallas import tpu as pltpu

F32 = jnp.float32
BF16 = jnp.bfloat16

D_MODEL = 1024
D_FF = 4 * D_MODEL
HEADS = 8
HEAD_DV = 128
CHUNK = 64
NORM_EPS = 1e-6
GATE_CAP = 15.0
GDN_CONV = 4
ROPE_THETA = 500000.0
DIFF_D = 64
ROT_DIMS = DIFF_D // 4
LANES = 128
HB = 2
GROUPS = HEADS // HB
VMEM_LIMIT = 56 * 1024 * 1024

_NT = (((1,), (1,)), ((), ()))
_TN = (((0,), (0,)), ((), ()))


def _dot(a, b):
    return jnp.dot(a, b, preferred_element_type=F32)


def _dot_nt(a, b):
    return lax.dot_general(a, b, _NT, preferred_element_type=F32)


def _dot_tn(a, b):
    return lax.dot_general(a, b, _TN, preferred_element_type=F32)


def _rms(x, g):
    ms = jnp.mean(x * x, axis=-1, keepdims=True)
    return x * lax.rsqrt(ms + NORM_EPS) * g


def _sigmoid(x):
    return 1.0 / (1.0 + jnp.exp(-x))


def _softplus(x):
    return jnp.maximum(x, 0.0) + jnp.log1p(jnp.exp(-jnp.abs(x)))


def _soft_cap(x):
    return GATE_CAP * jnp.tanh(x / GATE_CAP)


def _split3(x):
    x1 = x.astype(BF16)
    r1 = x - x1.astype(F32)
    x2 = r1.astype(BF16)
    r2 = r1 - x2.astype(F32)
    return x1, x2, r2.astype(BF16)


def _cumsum_rows(x, tril16):
    a, b, c = _split3(x)
    return _dot(tril16, a) + _dot(tril16, b) + _dot(tril16, c)


def _cumsum_lanes(x, triu16):
    a, b, c = _split3(x)
    return _dot(a, triu16) + _dot(b, triu16) + _dot(c, triu16)


def _split2(x):
    hi = x.astype(BF16)
    return hi, (x - hi.astype(F32)).astype(BF16)


def _dot3(a, b):
    ah, al = _split2(a)
    bh, bl = _split2(b)
    return _dot(ah, bh) + _dot(ah, bl) + _dot(al, bh)


def _unit_lower_inverse_minus_identity(a):
    n = a[0].shape[0]
    r = lax.broadcasted_iota(jnp.int32, (n, n), 0)
    c = lax.broadcasted_iota(jnp.int32, (n, n), 1)
    pair = jnp.right_shift(r, 1) == jnp.right_shift(c, 1)
    inv_m1 = [-jnp.where(pair, x, 0.0) for x in a]
    s = 2
    while s < n:
        sh = s.bit_length() - 1
        merge = ((jnp.right_shift(r, sh + 1) == jnp.right_shift(c, sh + 1))
                 & (jnp.right_shift(r, sh) != jnp.right_shift(c, sh)))
        off = [jnp.where(merge, x, 0.0) for x in a]
        t = [_dot3(o, m) for o, m in zip(off, inv_m1)]
        y = [o + x for o, x in zip(off, t)]
        t = [_dot3(m, x) for m, x in zip(inv_m1, y)]
        inv_m1 = [m - (x + z) for m, x, z in zip(inv_m1, y, t)]
        s *= 2
    return inv_m1


def _tri_masks(n):
    r = lax.broadcasted_iota(jnp.int32, (n, n), 0)
    c = lax.broadcasted_iota(jnp.int32, (n, n), 1)
    return r >= c, r > c, r <= c


def _ones_where(mask):
    return jnp.where(mask, 1.0, 0.0).astype(BF16)


def _in_proj_kernel(x_ref, g_ref, w_ref, o_ref, h_ref):
    @pl.when(pl.program_id(1) == 0)
    def _():
        h_ref[...] = _rms(x_ref[...], g_ref[...]).astype(BF16)

    o_ref[...] = _dot(h_ref[...], w_ref[...]).astype(o_ref.dtype)


def _in_proj_gates_kernel(x_ref, g_ref, w_ref, wc_ref, wr_ref, o_ref, oc_ref, or_ref, h_ref):
    @pl.when(pl.program_id(1) == 0)
    def _():
        h = _rms(x_ref[...], g_ref[...]).astype(BF16)
        h_ref[...] = h
        oc_ref[...] = _dot(h, wc_ref[...])
        or_ref[...] = _dot_nt(wr_ref[...], h)

    o_ref[...] = _dot(h_ref[...], w_ref[...]).astype(o_ref.dtype)


def _in_proj(x2, g, w16, wc16=None, wr16=None, *, tm=512, tn=512):
    m, d = x2.shape
    n = w16.shape[1]
    grid = (m // tm, n // tn)
    x_spec = pl.BlockSpec((tm, d), lambda i, j: (i, 0))
    g_spec = pl.BlockSpec((1, d), lambda i, j: (0, 0))
    w_spec = pl.BlockSpec((d, tn), lambda i, j: (0, j))
    o_spec = pl.BlockSpec((tm, tn), lambda i, j: (i, j))
    params = pltpu.CompilerParams(dimension_semantics=("parallel", "arbitrary"),
                                  vmem_limit_bytes=VMEM_LIMIT)
    if wc16 is None:
        return pl.pallas_call(
            _in_proj_kernel, grid=grid,
            in_specs=[x_spec, g_spec, w_spec], out_specs=o_spec,
            out_shape=jax.ShapeDtypeStruct((m, n), F32),
            scratch_shapes=[pltpu.VMEM((tm, d), BF16)],
            compiler_params=params, name="in_proj",
        )(x2, g, w16)
    nc = wc16.shape[1]
    nr = wr16.shape[0]
    return pl.pallas_call(
        _in_proj_gates_kernel, grid=grid,
        in_specs=[x_spec, g_spec, w_spec,
                  pl.BlockSpec((d, nc), lambda i, j: (0, 0)),
                  pl.BlockSpec((nr, d), lambda i, j: (0, 0))],
        out_specs=[o_spec,
                   pl.BlockSpec((tm, nc), lambda i, j: (i, 0)),
                   pl.BlockSpec((nr, tm), lambda i, j: (0, i))],
        out_shape=[jax.ShapeDtypeStruct((m, n), F32),
                   jax.ShapeDtypeStruct((m, nc), F32),
                   jax.ShapeDtypeStruct((nr, m), F32)],
        scratch_shapes=[pltpu.VMEM((tm, d), BF16)],
        compiler_params=params, name="in_proj_gates",
    )(x2, g, w16, wc16, wr16)


def _gate_weights(w_a, w_b):
    d = w_a.shape[0]
    a = w_a.reshape(d, GROUPS, HB)
    b = w_b.reshape(d, GROUPS, HB)
    pad = jnp.zeros((d, GROUPS, LANES - 2 * HB), w_a.dtype)
    wc = jnp.concatenate([a, b, pad], axis=2).reshape(d, GROUPS * LANES)
    wr = jnp.concatenate([w_a, w_b], axis=1).T
    return wc.astype(BF16), wr.astype(BF16)


def _group_rows(p_a, p_b):
    pad = jnp.zeros((GROUPS, LANES - HB), F32)
    ra = jnp.concatenate([p_a.reshape(GROUPS, HB), pad], axis=1)
    rb = jnp.concatenate([p_b.reshape(GROUPS, HB), pad], axis=1)
    return jnp.stack([ra, rb], axis=1)


def _gdn_kernel(q_ref, k_ref, v_ref, z_ref, gc_ref, gr_ref, cwq_ref, cwk_ref, cwv_ref,
                pcol_ref, prow_ref, ng_ref, o_ref,
                xp_ref, qn_ref, kn_ref, vn_ref, bcol_ref, gcol_ref, grow_ref,
                val_ref, kq_ref, att_ref, kst_ref, *, seq):
    nch = seq // CHUNK
    grp = pl.program_id(1)
    tril, strict, triu = _tri_masks(CHUNK)
    tril16 = _ones_where(tril)
    triu16 = _ones_where(triu)

    alog_row = pcol_ref[0, 0:1, :]
    dtb_row = pcol_ref[0, 1:2, :]
    neg_decay_rate = -jnp.exp(alog_row)

    def gate_cols(c, carry):
        rows = pl.ds(pl.multiple_of(c * CHUNK, CHUNK), CHUNK)
        ga = gc_ref[rows, :]
        g = neg_decay_rate * _softplus(ga + dtb_row)
        gcol_ref[rows, :] = _cumsum_rows(g, tril16)
        bcol_ref[rows, :] = _sigmoid(ga)
        return carry

    lax.fori_loop(0, nch, gate_cols, 0, unroll=4)

    for i in range(HB):
        hh = grp * HB + i
        a_r = gr_ref[hh, 0]
        alog = jnp.full((nch, CHUNK), prow_ref[0, hh], F32)
        g_r = -jnp.exp(alog) * _softplus(a_r + prow_ref[1, hh])
        grow_ref[i] = _cumsum_lanes(g_r, triu16)

    tile = 256
    xp_ref[0:8, :] = jnp.zeros((8, LANES), F32)
    jobs = []
    for i in range(HB):
        cs = slice(i * LANES, (i + 1) * LANES)
        jobs.append((q_ref, cwq_ref, qn_ref, cs, HEAD_DV ** -0.5))
        jobs.append((k_ref, cwk_ref, kn_ref, cs, 1.0))
        jobs.append((v_ref, cwv_ref, vn_ref, cs, None))
    for src, cw_ref, dst, cs, scale in jobs:
        xp_ref[8:seq + 8, :] = src[:, cs]
        cw = cw_ref[:, cs]

        def conv_tile(t, carry, dst=dst, cs=cs, cw=cw, scale=scale):
            r0 = pl.multiple_of(t * tile, tile)
            y = cw[3:4, :] * xp_ref[pl.ds(r0 + 8, tile), :]
            for j in range(GDN_CONV - 1):
                y = y + cw[j:j + 1, :] * xp_ref[pl.ds(r0 + 5 + j, tile), :]
            y = y * _sigmoid(y)
            if scale is not None:
                y = y * lax.rsqrt(jnp.sum(y * y, axis=-1, keepdims=True) + NORM_EPS)
                if scale != 1.0:
                    y = y * scale
            dst[pl.ds(r0, tile), cs] = y
            return carry

        lax.fori_loop(0, seq // tile, conv_tile, 0)

    chunks_per_iter = 4

    def intra(cc, carry):
        items = []
        for u in range(chunks_per_iter):
            c = cc * chunks_per_iter + u
            rows = pl.ds(pl.multiple_of(c * CHUNK, CHUNK), CHUNK)
            for i in range(HB):
                items.append((c, rows, i, slice(i * LANES, (i + 1) * LANES)))
        q = [qn_ref[rows, cs] for _, rows, _, cs in items]
        k = [kn_ref[rows, cs] for _, rows, _, cs in items]
        v = [vn_ref[rows, cs] for _, rows, _, cs in items]
        beta = [bcol_ref[rows, HB + i:HB + i + 1] for _, rows, i, _ in items]
        gc = [gcol_ref[rows, i:i + 1] for _, rows, i, _ in items]
        gr = [grow_ref[i, pl.ds(c, 1), :] for c, _, i, _ in items]
        decay = [jnp.where(tril, jnp.exp(jnp.where(tril, a - b, 0.0)), 0.0) for a, b in zip(gc, gr)]
        kb = [a * b for a, b in zip(k, beta)]
        k16 = [a.astype(BF16) for a in k]
        g_kk = [_dot_nt(a.astype(BF16), b) for a, b in zip(kb, k16)]
        g_qk = [_dot_nt(a.astype(BF16), b) for a, b in zip(q, k16)]
        a_mat = [jnp.where(strict, a * d, 0.0) for a, d in zip(g_kk, decay)]
        inv_m1 = _unit_lower_inverse_minus_identity(a_mat)
        egc = [jnp.exp(a) for a in gc]
        rhs = [jnp.concatenate([a * b, c_ * e], axis=1)
               for a, b, c_, e in zip(v, beta, kb, egc)]
        sol = [a + _dot3(m, a) for m, a in zip(inv_m1, rhs)]
        for n, (c, rows, i, cs) in enumerate(items):
            val_ref[rows, cs] = sol[n][:, :LANES]
            kq_rows = pl.ds(pl.multiple_of(c * (2 * CHUNK), 2 * CHUNK), CHUNK)
            qi_rows = pl.ds(pl.multiple_of(c * (2 * CHUNK) + CHUNK, CHUNK), CHUNK)
            kq_ref[kq_rows, cs] = sol[n][:, LANES:].astype(BF16)
            kq_ref[qi_rows, cs] = (q[n] * egc[n]).astype(BF16)
            att_ref[i, rows, :] = jnp.where(tril, g_qk[n] * decay[n], 0.0).astype(BF16)
            gl = gc[n][CHUNK - 1:CHUNK, :]
            kst_ref[rows, cs] = (k[n] * jnp.exp(gl - gc[n])).astype(BF16)
        return carry

    lax.fori_loop(0, nch // chunks_per_iter, intra, 0)

    ng = ng_ref[...]

    def step(c, states):
        r0 = pl.multiple_of(c * CHUNK, CHUNK)
        rows = pl.ds(r0, CHUNK)
        heads = range(HB)
        cols = [slice(i * LANES, (i + 1) * LANES) for i in heads]
        kq_rows = pl.ds(pl.multiple_of(c * (2 * CHUNK), 2 * CHUNK), 2 * CHUNK)
        r = [_dot(kq_ref[kq_rows, cols[i]], states[i].astype(BF16)) for i in heads]
        vn16 = [(val_ref[rows, cols[i]] - r[i][:CHUNK]).astype(BF16) for i in heads]
        o_intra = [_dot(att_ref[i, rows, :], vn16[i]) for i in heads]
        upd = [_dot_tn(kst_ref[rows, cols[i]], vn16[i]) for i in heads]
        new_states = []
        for i in heads:
            egl = jnp.exp(gcol_ref[pl.ds(r0 + CHUNK - 1, 1), i:i + 1])
            new_states.append(states[i] * egl + upd[i])
            z = z_ref[rows, cols[i]]
            o = r[i][CHUNK:] + o_intra[i]
            o_ref[rows, cols[i]] = (_rms(o, ng) * (z * _sigmoid(z))).astype(o_ref.dtype)
        return tuple(new_states)

    init = tuple(jnp.zeros((HEAD_DV, HEAD_DV), F32) for _ in range(HB))
    lax.fori_loop(0, nch, step, init)


def _gdn_core(proj, ga, gat, conv_w, a_log, dt_bias, norm_g, *, batch, seq):
    m = batch * seq
    nch = seq // CHUNK
    wb = HB * LANES
    g_ = GROUPS
    gat4 = gat.reshape(2 * HEADS, batch, nch, CHUNK)
    pcol = _group_rows(a_log, dt_bias)
    prow = jnp.stack([a_log, dt_bias], axis=0)
    kernel = functools.partial(_gdn_kernel, seq=seq)
    return pl.pallas_call(
        kernel, grid=(batch, g_),
        in_specs=[
            pl.BlockSpec((seq, wb), lambda b, g: (b, g)),
            pl.BlockSpec((seq, wb), lambda b, g: (b, g_ + g)),
            pl.BlockSpec((seq, wb), lambda b, g: (b, 2 * g_ + g)),
            pl.BlockSpec((seq, wb), lambda b, g: (b, 3 * g_ + g)),
            pl.BlockSpec((seq, LANES), lambda b, g: (b, g)),
            pl.BlockSpec((2 * HEADS, 1, nch, CHUNK), lambda b, g: (0, b, 0, 0)),
            pl.BlockSpec((GDN_CONV, wb), lambda b, g: (0, g)),
            pl.BlockSpec((GDN_CONV, wb), lambda b, g: (0, g_ + g)),
            pl.BlockSpec((GDN_CONV, wb), lambda b, g: (0, 2 * g_ + g)),
            pl.BlockSpec((1, 2, LANES), lambda b, g: (g, 0, 0)),
            pl.BlockSpec(memory_space=pltpu.SMEM),
            pl.BlockSpec((1, LANES), lambda b, g: (0, 0)),
        ],
        out_specs=pl.BlockSpec((seq, wb), lambda b, g: (b, g)),
        out_shape=jax.ShapeDtypeStruct((m, HEADS * HEAD_DV), BF16),
        scratch_shapes=[
            pltpu.VMEM((seq + 8, LANES), F32),
            pltpu.VMEM((seq, wb), F32),
            pltpu.VMEM((seq, wb), F32),
            pltpu.VMEM((seq, wb), F32),
            pltpu.VMEM((seq, LANES), F32),
            pltpu.VMEM((seq, LANES), F32),
            pltpu.VMEM((HB, nch, CHUNK), F32),
            pltpu.VMEM((seq, wb), F32),
            pltpu.VMEM((2 * seq, wb), BF16),
            pltpu.VMEM((HB, seq, CHUNK), BF16),
            pltpu.VMEM((seq, wb), BF16),
        ],
        compiler_params=pltpu.CompilerParams(dimension_semantics=("parallel", "arbitrary"),
                                             vmem_limit_bytes=VMEM_LIMIT),
        name="gdn_core",
    )(proj, proj, proj, proj, ga, gat4, conv_w, conv_w, conv_w, pcol, prow,
      norm_g.reshape(1, HEAD_DV))


def _mlstm_kernel(q_ref, k_ref, v_ref, og_ref, gc_ref, gr_ref, pcol_ref, prow_ref, ng_ref,
                  o_ref, icol_ref, bcol_ref, drow_ref, c_ref, *, seq):
    nch = seq // CHUNK
    dk = HEAD_DV // 2
    grp = pl.program_id(1)
    tril, _, triu = _tri_masks(CHUNK)
    tril16 = _ones_where(tril)
    triu16 = _ones_where(triu)
    bias_row = pcol_ref[0, 0:1, :]

    def gate_cols(c, carry):
        rows = pl.ds(pl.multiple_of(c * CHUNK, CHUNK), CHUNK)
        pre = _soft_cap(gc_ref[rows, :] + bias_row)
        icol_ref[rows, :] = pre
        bcol_ref[rows, :] = _cumsum_rows(-_softplus(-pre), tril16)
        return carry

    lax.fori_loop(0, nch, gate_cols, 0, unroll=4)

    for i in range(HB):
        hh = grp * HB + i
        i_r = _soft_cap(gr_ref[hh, 0] + prow_ref[0, hh])
        f_r = _soft_cap(gr_ref[HEADS + hh, 0] + prow_ref[1, hh])
        drow_ref[i] = i_r - _cumsum_lanes(-_softplus(-f_r), triu16)

    lane = lax.broadcasted_iota(jnp.int32, (CHUNK, LANES), 1)
    ones_col = _ones_where(lane == 0)
    c_ref[...] = jnp.zeros(c_ref.shape, F32)

    chunks_per_iter = 2
    head_lanes = [(lane >= i * dk) & (lane < (i + 1) * dk) for i in range(HB)]
    cols = [slice(i * LANES, (i + 1) * LANES) for i in range(HB)]

    def step(cc, m_states):
        m_st = list(m_states)
        items = [(u, i) for u in range(chunks_per_iter) for i in range(HB)]
        chunk = [cc * chunks_per_iter + u for u in range(chunks_per_iter)]
        rows = [pl.ds(pl.multiple_of(c * CHUNK, CHUNK), CHUNK) for c in chunk]
        qp = [q_ref[r, :] for r in rows]
        kp = [k_ref[r, :] for r in rows]
        kp16 = [x.astype(BF16) for x in kp]
        bcum = [bcol_ref[rows[u], HB + i:HB + i + 1] for u, i in items]
        ig = [icol_ref[rows[u], i:i + 1] for u, i in items]
        dlog = [jnp.where(tril, b + drow_ref[i, pl.ds(chunk[u], 1), :], -jnp.inf)
                for b, (u, i) in zip(bcum, items)]
        m_intra = [jnp.max(d, axis=-1, keepdims=True) for d in dlog]
        q16 = [(jnp.where(head_lanes[i], qp[u], 0.0) * dk ** -0.5).astype(BF16) for u, i in items]
        s_raw = [_dot_nt(q, kp16[u]) for q, (u, i) in zip(q16, items)]
        g = [b[CHUNK - 1:CHUNK, :] for b in bcum]
        a = [g_ - b + i_ for g_, b, i_ in zip(g, bcum, ig)]
        a_max = [jnp.max(x, axis=0, keepdims=True) for x in a]
        m_out, w_inter, w_old, m_new = [], [], [], []
        for n, (u, i) in enumerate(items):
            m_inter = bcum[n] + m_st[i]
            m_out.append(jnp.maximum(m_inter, m_intra[n]))
            w_inter.append(jnp.exp(m_inter - m_out[n]))
            m_new.append(jnp.maximum(g[n] + m_st[i], a_max[n]))
            w_old.append(jnp.exp(g[n] + m_st[i] - m_new[n]))
            m_st[i] = m_new[n]
        vaug = [jnp.concatenate([v_ref[rows[u], cols[i]].astype(BF16), ones_col], axis=1)
                for u, i in items]
        s16 = [(s * jnp.exp(d - m)).astype(BF16) for s, d, m in zip(s_raw, dlog, m_out)]
        sv = [_dot(s, v) for s, v in zip(s16, vaug)]
        kw = [(jnp.where(head_lanes[i], kp[u], 0.0) * jnp.exp(a[n] - m_new[n])).astype(BF16)
              for n, (u, i) in enumerate(items)]
        kv = [_dot_tn(k_, v) for k_, v in zip(kw, vaug)]
        cst = [c_ref[i] for i in range(HB)]
        for n, (u, i) in enumerate(items):
            nd = w_inter[n] * _dot(q16[n], cst[i].astype(BF16)) + sv[n]
            den = nd[:, LANES:LANES + 1]
            h = nd[:, :LANES] * (1.0 / jnp.maximum(jnp.abs(den), jnp.exp(-m_out[n])))
            cst[i] = w_old[n] * cst[i] + kv[n]
            hn = _rms(h, ng_ref[:, cols[i]])
            o_ref[rows[u], cols[i]] = (_sigmoid(og_ref[rows[u], cols[i]]) * hn).astype(o_ref.dtype)
        for i in range(HB):
            c_ref[i] = cst[i]
        return tuple(m_st)

    lax.fori_loop(0, nch // chunks_per_iter, step,
                  tuple(jnp.zeros((1, 1), F32) for _ in range(HB)))


def _mlstm_core(proj, ga, gat, gate_b, norm_g, *, batch, seq):
    m = batch * seq
    nch = seq // CHUNK
    wb = HB * LANES
    g_ = GROUPS
    gat4 = gat.reshape(2 * HEADS, batch, nch, CHUNK)
    pad = jnp.zeros((GROUPS, LANES - 2 * HB), F32)
    pcol = jnp.concatenate([gate_b[0].reshape(GROUPS, HB), gate_b[1].reshape(GROUPS, HB), pad],
                           axis=1).reshape(GROUPS, 1, LANES)
    kernel = functools.partial(_mlstm_kernel, seq=seq)
    return pl.pallas_call(
        kernel, grid=(batch, g_),
        in_specs=[
            pl.BlockSpec((seq, LANES), lambda b, g: (b, g)),
            pl.BlockSpec((seq, LANES), lambda b, g: (b, g_ + g)),
            pl.BlockSpec((seq, wb), lambda b, g: (b, g_ + g)),
            pl.BlockSpec((seq, wb), lambda b, g: (b, 2 * g_ + g)),
            pl.BlockSpec((seq, LANES), lambda b, g: (b, g)),
            pl.BlockSpec((2 * HEADS, 1, nch, CHUNK), lambda b, g: (0, b, 0, 0)),
            pl.BlockSpec((1, 1, LANES), lambda b, g: (g, 0, 0)),
            pl.BlockSpec(memory_space=pltpu.SMEM),
            pl.BlockSpec((1, wb), lambda b, g: (0, g)),
        ],
        out_specs=pl.BlockSpec((seq, wb), lambda b, g: (b, g)),
        out_shape=jax.ShapeDtypeStruct((m, HEADS * HEAD_DV), BF16),
        scratch_shapes=[
            pltpu.VMEM((seq, LANES), F32),
            pltpu.VMEM((seq, LANES), F32),
            pltpu.VMEM((HB, nch, CHUNK), F32),
            pltpu.VMEM((HB, LANES, 2 * LANES), F32),
        ],
        compiler_params=pltpu.CompilerParams(dimension_semantics=("parallel", "arbitrary"),
                                             vmem_limit_bytes=VMEM_LIMIT),
        name="mlstm_core",
    )(proj, proj, proj, proj, ga, gat4, pcol, gate_b, norm_g.reshape(1, HEADS * HEAD_DV))


def _rope(x, cos, sin_lo, sin_hi):
    half = ROT_DIMS // 2
    return (x * cos + pltpu.roll(x, LANES - half, axis=1) * sin_lo
            + pltpu.roll(x, half, axis=1) * sin_hi)


def _diff_kernel(q_ref, k_ref, v_ref, tq_ref, tk_ref, lam_ref, ng_ref, o_ref,
                 kr_ref, vt_ref, q2_ref, m_ref, l_ref, acc_ref, *, seq, tq, dh, lambda_init):
    qi = pl.program_id(2)
    tk = tq

    @pl.when(qi == 0)
    def _():
        def prep_tile(t, carry):
            rows = pl.ds(pl.multiple_of(t * tk, tk), tk)
            for d in range(dh):
                cs = slice(d * LANES, (d + 1) * LANES)
                kr_ref[d, rows, :] = _rope(k_ref[rows, cs], tk_ref[0, rows, :], tk_ref[1, rows, :],
                                           tk_ref[2, rows, :]).astype(BF16)
                vt_ref[d, :, rows] = v_ref[rows, cs].T.astype(BF16)
            return carry
        lax.fori_loop(0, seq // tk, prep_tile, 0)

    lane = lax.broadcasted_iota(jnp.int32, (tq, LANES), 1)
    for d in range(dh):
        cs = slice(d * LANES, (d + 1) * LANES)
        q = _rope(q_ref[:, cs], tq_ref[0], tq_ref[1], tq_ref[2]) * DIFF_D ** -0.5
        q2_ref[d] = jnp.concatenate(
            [jnp.where(lane < DIFF_D, q, 0.0), jnp.where(lane >= DIFF_D, q, 0.0)],
            axis=0).astype(BF16)
    m_ref[...] = jnp.full(m_ref.shape, -jnp.inf, F32)
    l_ref[...] = jnp.zeros(l_ref.shape, F32)
    acc_ref[...] = jnp.zeros(acc_ref.shape, F32)

    def kv_block(j, masked):
        rows = pl.ds(pl.multiple_of(j * tk, tk), tk)
        st = [_dot_nt(kr_ref[d, rows, :], q2_ref[d]) for d in range(dh)]
        if masked:
            kpos = lax.broadcasted_iota(jnp.int32, (tk, 2 * tq), 0)
            qpos = lax.broadcasted_iota(jnp.int32, (tk, 2 * tq), 1)
            qpos = jnp.where(qpos >= tq, qpos - tq, qpos)
            shift = CHUNK.bit_length() - 1
            keep = jnp.right_shift(kpos, shift) <= jnp.right_shift(qpos, shift)
            st = [jnp.where(keep, s, -jnp.inf) for s in st]
        m_prev = [m_ref[d] for d in range(dh)]
        m_new = [jnp.maximum(m, jnp.max(s, axis=0, keepdims=True))
                 for m, s in zip(m_prev, st)]
        p = [jnp.exp(s - m) for s, m in zip(st, m_new)]
        pv = [_dot(vt_ref[d, :, rows], p[d].astype(BF16)) for d in range(dh)]
        for d in range(dh):
            alpha = jnp.exp(m_prev[d] - m_new[d])
            l_ref[d] = alpha * l_ref[d] + jnp.sum(p[d], axis=0, keepdims=True)
            acc_ref[d] = alpha * acc_ref[d] + pv[d]
            m_ref[d] = m_new[d]

    def body(j, carry):
        kv_block(j, False)
        return carry

    lax.fori_loop(0, qi, body, 0)
    kv_block(qi, True)

    lp = lam_ref[...]
    lam = (jnp.exp(jnp.sum(lp[0:1] * lp[1:2], axis=-1, keepdims=True))
           - jnp.exp(jnp.sum(lp[2:3] * lp[3:4], axis=-1, keepdims=True)) + lambda_init)
    for d in range(dh):
        out_t = acc_ref[d] * (1.0 / l_ref[d])
        o = (out_t[:, :tq] - lam * out_t[:, tq:]).T
        o_ref[:, d * LANES:(d + 1) * LANES] = (
            _rms(o, ng_ref[...]) * (1.0 - lambda_init)).astype(o_ref.dtype)


def _rope_tables(seq):
    half = ROT_DIMS // 2
    inv_freq = ROPE_THETA ** (-jnp.arange(0, ROT_DIMS, 2, dtype=F32) / ROT_DIMS)
    ang = jnp.arange(seq, dtype=F32)[:, None] * inv_freq[None, :]
    cos, sin = jnp.cos(ang), jnp.sin(ang)
    one = jnp.ones((seq, DIFF_D - ROT_DIMS), F32)
    zero = jnp.zeros((seq, DIFF_D - ROT_DIMS), F32)
    zh = jnp.zeros((seq, half), F32)
    cos_c = jnp.concatenate([cos, cos, one], axis=1)
    lo_c = jnp.concatenate([-sin, zh, zero], axis=1)
    hi_c = jnp.concatenate([zh, sin, zero], axis=1)
    rep = lambda t: jnp.concatenate([t, t], axis=1)
    return jnp.stack([rep(cos_c), rep(lo_c), rep(hi_c)], axis=0)


def _diff_core(proj, tables, lam_p, norm_g, lambda_init, *, batch, seq, tq=256, dh=2):
    m = batch * seq
    nq = seq // tq
    hg = HEADS // dh
    wb = dh * LANES
    kernel = functools.partial(_diff_kernel, seq=seq, tq=tq, dh=dh, lambda_init=lambda_init)
    return pl.pallas_call(
        kernel, grid=(batch, hg, nq),
        in_specs=[
            pl.BlockSpec((tq, wb), lambda b, h, i: (b * nq + i, h)),
            pl.BlockSpec((seq, wb), lambda b, h, i: (b, hg + h)),
            pl.BlockSpec((seq, wb), lambda b, h, i: (b, 2 * hg + h)),
            pl.BlockSpec((3, tq, LANES), lambda b, h, i: (0, i, 0)),
            pl.BlockSpec((3, seq, LANES), lambda b, h, i: (0, 0, 0)),
            pl.BlockSpec((4, DIFF_D), lambda b, h, i: (0, 0)),
            pl.BlockSpec((1, LANES), lambda b, h, i: (0, 0)),
        ],
        out_specs=pl.BlockSpec((tq, wb), lambda b, h, i: (b * nq + i, h)),
        out_shape=jax.ShapeDtypeStruct((m, HEADS * 2 * DIFF_D), BF16),
        scratch_shapes=[
            pltpu.VMEM((dh, seq, LANES), BF16),
            pltpu.VMEM((dh, LANES, seq), BF16),
            pltpu.VMEM((dh, 2 * tq, LANES), BF16),
            pltpu.VMEM((dh, 1, 2 * tq), F32),
            pltpu.VMEM((dh, 1, 2 * tq), F32),
            pltpu.VMEM((dh, LANES, 2 * tq), F32),
        ],
        compiler_params=pltpu.CompilerParams(
            dimension_semantics=("parallel", "parallel", "arbitrary"),
            vmem_limit_bytes=VMEM_LIMIT),
        name="diff_core",
    )(proj, proj, proj, tables, tables, lam_p, norm_g.reshape(1, 2 * DIFF_D))


def _out_proj_kernel(o_ref, w_ref, g_ref, x_ref, y_ref):
    y_ref[...] = x_ref[...] + _rms(_dot(o_ref[...], w_ref[...]), g_ref[...])


def _out_proj(o16, w16, g, x2, *, tm=512):
    m, d = x2.shape
    return pl.pallas_call(
        _out_proj_kernel, grid=(m // tm,),
        in_specs=[pl.BlockSpec((tm, d), lambda i: (i, 0)),
                  pl.BlockSpec((d, d), lambda i: (0, 0)),
                  pl.BlockSpec((1, d), lambda i: (0, 0)),
                  pl.BlockSpec((tm, d), lambda i: (i, 0))],
        out_specs=pl.BlockSpec((tm, d), lambda i: (i, 0)),
        out_shape=jax.ShapeDtypeStruct((m, d), F32),
        compiler_params=pltpu.CompilerParams(dimension_semantics=("parallel",),
                                             vmem_limit_bytes=VMEM_LIMIT),
        name="out_proj",
    )(o16, w16, g, x2)


def _mlp_kernel(x_ref, g2_ref, w1_ref, w2_ref, g3_ref, y_ref, h_ref, acc_ref):
    j = pl.program_id(1)

    @pl.when(j == 0)
    def _():
        h_ref[...] = _rms(x_ref[...], g2_ref[...]).astype(BF16)
        acc_ref[...] = jnp.zeros(acc_ref.shape, F32)

    a = jnp.maximum(_dot(h_ref[...], w1_ref[...]), 0.0)
    acc_ref[...] += _dot((a * a).astype(BF16), w2_ref[...])

    @pl.when(j == pl.num_programs(1) - 1)
    def _():
        y_ref[...] = x_ref[...] + _rms(acc_ref[...], g3_ref[...])


def _mlp(x2, g2, w1_16, w2_16, g3, *, tm=512, tf=512):
    m, d = x2.shape
    f = w1_16.shape[1]
    return pl.pallas_call(
        _mlp_kernel, grid=(m // tm, f // tf),
        in_specs=[pl.BlockSpec((tm, d), lambda i, j: (i, 0)),
                  pl.BlockSpec((1, d), lambda i, j: (0, 0)),
                  pl.BlockSpec((d, tf), lambda i, j: (0, j)),
                  pl.BlockSpec((tf, d), lambda i, j: (j, 0)),
                  pl.BlockSpec((1, d), lambda i, j: (0, 0))],
        out_specs=pl.BlockSpec((tm, d), lambda i, j: (i, 0)),
        out_shape=jax.ShapeDtypeStruct((m, d), F32),
        scratch_shapes=[pltpu.VMEM((tm, d), BF16), pltpu.VMEM((tm, d), F32)],
        compiler_params=pltpu.CompilerParams(dimension_semantics=("parallel", "arbitrary"),
                                             vmem_limit_bytes=VMEM_LIMIT),
        name="mlp",
    )(x2, g2, w1_16, w2_16, g3)


def kernel(x, norm_g, mlp_w1, mlp_w2, gdn_w_in, gdn_conv, gdn_a_log, gdn_dt_bias, gdn_norm_g,
           gdn_w_out, mlstm_w_in, mlstm_gate_b, mlstm_norm_g, mlstm_w_out, diff_w_in,
           diff_lambda, diff_norm_g, diff_w_out):
    batch, seq, d = x.shape
    depth = norm_g.shape[0]
    x2 = x.reshape(batch * seq, d)
    tables = _rope_tables(seq)
    row = lambda v: v.reshape(1, -1)
    for i in range(depth):
        kind, j = i % 3, i // 3
        g_pre = row(norm_g[i, 0])
        if kind == 0:
            w_in = gdn_w_in[j]
            n_main = 4 * HEADS * HEAD_DV
            wc, wr = _gate_weights(w_in[:, n_main:n_main + HEADS], w_in[:, n_main + HEADS:])
            proj, ga, gat = _in_proj(x2, g_pre, w_in[:, :n_main].astype(BF16), wc, wr)
            o16 = _gdn_core(proj, ga, gat, gdn_conv[j], gdn_a_log[j], gdn_dt_bias[j],
                            gdn_norm_g[j], batch=batch, seq=seq)
            w_out = gdn_w_out[j]
        elif kind == 1:
            w_in = mlstm_w_in[j]
            n_main = 3 * HEADS * HEAD_DV
            wc, wr = _gate_weights(w_in[:, n_main:n_main + HEADS], w_in[:, n_main + HEADS:])
            proj, ga, gat = _in_proj(x2, g_pre, w_in[:, :n_main].astype(BF16), wc, wr)
            o16 = _mlstm_core(proj, ga, gat, mlstm_gate_b[j], mlstm_norm_g[j],
                              batch=batch, seq=seq)
            w_out = mlstm_w_out[j]
        else:
            lambda_init = 0.8 - 0.6 * math.exp(-0.3 * i)
            proj = _in_proj(x2, g_pre, diff_w_in[j].astype(BF16))
            o16 = _diff_core(proj, tables, diff_lambda[j], diff_norm_g[j], lambda_init,
                             batch=batch, seq=seq)
            w_out = diff_w_out[j]
        x2 = _out_proj(o16, w_out.astype(BF16), row(norm_g[i, 1]), x2)
        x2 = _mlp(x2, row(norm_g[i, 2]), mlp_w1[i].astype(BF16), mlp_w2[i].astype(BF16),
                  row(norm_g[i, 3]))
    return x2.reshape(batch, seq, d)
```

```python
import functools
import math

import jax
import jax.numpy as jnp
from jax import lax
from jax.experimental import pallas as pl
from jax.experimental.pallas import tpu as pltpu

F32 = jnp.float32
BF16 = jnp.bfloat16

D_MODEL = 1024
D_FF = 4 * D_MODEL
HEADS = 8
HEAD_DV = 128
CHUNK = 64
NORM_EPS = 1e-6
GATE_CAP = 15.0
GDN_CONV = 4
ROPE_THETA = 500000.0
DIFF_D = 64
ROT_DIMS = DIFF_D // 4
LANES = 128
HB = 2
GROUPS = HEADS // HB
VMEM_LIMIT = 56 * 1024 * 1024

_NT = (((1,), (1,)), ((), ()))
_TN = (((0,), (0,)), ((), ()))


def _dot(a, b):
    return jnp.dot(a, b, preferred_element_type=F32)


def _dot_nt(a, b):
    return lax.dot_general(a, b, _NT, preferred_element_type=F32)


def _dot_tn(a, b):
    return lax.dot_general(a, b, _TN, preferred_element_type=F32)


def _rms(x, g):
    ms = jnp.mean(x * x, axis=-1, keepdims=True)
    return x * lax.rsqrt(ms + NORM_EPS) * g


def _sigmoid(x):
    return 1.0 / (1.0 + jnp.exp(-x))


def _softplus(x):
    return jnp.maximum(x, 0.0) + jnp.log1p(jnp.exp(-jnp.abs(x)))


def _soft_cap(x):
    return GATE_CAP * jnp.tanh(x / GATE_CAP)


def _split3(x):
    x1 = x.astype(BF16)
    r1 = x - x1.astype(F32)
    x2 = r1.astype(BF16)
    r2 = r1 - x2.astype(F32)
    return x1, x2, r2.astype(BF16)


def _cumsum_rows(x, tril16):
    a, b, c = _split3(x)
    return _dot(tril16, a) + _dot(tril16, b) + _dot(tril16, c)


def _cumsum_lanes(x, triu16):
    a, b, c = _split3(x)
    return _dot(a, triu16) + _dot(b, triu16) + _dot(c, triu16)


def _split2(x):
    hi = x.astype(BF16)
    return hi, (x - hi.astype(F32)).astype(BF16)


def _dot3(a, b):
    ah, al = _split2(a)
    bh, bl = _split2(b)
    return _dot(ah, bh) + _dot(ah, bl) + _dot(al, bh)


def _unit_lower_inverse_minus_identity(a):
    n = a[0].shape[0]
    r = lax.broadcasted_iota(jnp.int32, (n, n), 0)
    c = lax.broadcasted_iota(jnp.int32, (n, n), 1)
    pair = jnp.right_shift(r, 1) == jnp.right_shift(c, 1)
    inv_m1 = [-jnp.where(pair, x, 0.0) for x in a]
    s = 2
    while s < n:
        sh = s.bit_length() - 1
        merge = ((jnp.right_shift(r, sh + 1) == jnp.right_shift(c, sh + 1))
                 & (jnp.right_shift(r, sh) != jnp.right_shift(c, sh)))
        off = [jnp.where(merge, x, 0.0) for x in a]
        t = [_dot3(o, m) for o, m in zip(off, inv_m1)]
        y = [o + x for o, x in zip(off, t)]
        t = [_dot3(m, x) for m, x in zip(inv_m1, y)]
        inv_m1 = [m - (x + z) for m, x, z in zip(inv_m1, y, t)]
        s *= 2
    return inv_m1


def _tri_masks(n):
    r = lax.broadcasted_iota(jnp.int32, (n, n), 0)
    c = lax.broadcasted_iota(jnp.int32, (n, n), 1)
    return r >= c, r > c, r <= c


def _ones_where(mask):
    return jnp.where(mask, 1.0, 0.0).astype(BF16)


def _in_proj_kernel(x_ref, g_ref, w_ref, o_ref, h_ref):
    @pl.when(pl.program_id(1) == 0)
    def _():
        h_ref[...] = _rms(x_ref[...], g_ref[...]).astype(BF16)

    o_ref[...] = _dot(h_ref[...], w_ref[...]).astype(o_ref.dtype)


def _in_proj_gates_kernel(x_ref, g_ref, w_ref, wc_ref, wr_ref, o_ref, oc_ref, or_ref, h_ref):
    @pl.when(pl.program_id(1) == 0)
    def _():
        h = _rms(x_ref[...], g_ref[...]).astype(BF16)
        h_ref[...] = h
        oc_ref[...] = _dot(h, wc_ref[...])
        or_ref[...] = _dot_nt(wr_ref[...], h)

    o_ref[...] = _dot(h_ref[...], w_ref[...]).astype(o_ref.dtype)


def _in_proj(x2, g, w16, wc16=None, wr16=None, *, tm=1024, tn=1024):
    m, d = x2.shape
    n = w16.shape[1]
    grid = (m // tm, n // tn)
    x_spec = pl.BlockSpec((tm, d), lambda i, j: (i, 0))
    g_spec = pl.BlockSpec((1, d), lambda i, j: (0, 0))
    w_spec = pl.BlockSpec((d, tn), lambda i, j: (0, j))
    o_spec = pl.BlockSpec((tm, tn), lambda i, j: (i, j))
    params = pltpu.CompilerParams(dimension_semantics=("parallel", "arbitrary"),
                                  vmem_limit_bytes=VMEM_LIMIT)
    if wc16 is None:
        return pl.pallas_call(
            _in_proj_kernel, grid=grid,
            in_specs=[x_spec, g_spec, w_spec], out_specs=o_spec,
            out_shape=jax.ShapeDtypeStruct((m, n), F32),
            scratch_shapes=[pltpu.VMEM((tm, d), BF16)],
            compiler_params=params, name="in_proj",
        )(x2, g, w16)
    nc = wc16.shape[1]
    nr = wr16.shape[0]
    return pl.pallas_call(
        _in_proj_gates_kernel, grid=grid,
        in_specs=[x_spec, g_spec, w_spec,
                  pl.BlockSpec((d, nc), lambda i, j: (0, 0)),
                  pl.BlockSpec((nr, d), lambda i, j: (0, 0))],
        out_specs=[o_spec,
                   pl.BlockSpec((tm, nc), lambda i, j: (i, 0)),
                   pl.BlockSpec((nr, tm), lambda i, j: (0, i))],
        out_shape=[jax.ShapeDtypeStruct((m, n), F32),
                   jax.ShapeDtypeStruct((m, nc), F32),
                   jax.ShapeDtypeStruct((nr, m), F32)],
        scratch_shapes=[pltpu.VMEM((tm, d), BF16)],
        compiler_params=params, name="in_proj_gates",
    )(x2, g, w16, wc16, wr16)


def _gate_weights(w_a, w_b):
    d = w_a.shape[0]
    a = w_a.reshape(d, GROUPS, HB)
    b = w_b.reshape(d, GROUPS, HB)
    pad = jnp.zeros((d, GROUPS, LANES - 2 * HB), w_a.dtype)
    wc = jnp.concatenate([a, b, pad], axis=2).reshape(d, GROUPS * LANES)
    wr = jnp.concatenate([w_a, w_b], axis=1).T
    return wc.astype(BF16), wr.astype(BF16)


def _group_rows(p_a, p_b):
    pad = jnp.zeros((GROUPS, LANES - HB), F32)
    ra = jnp.concatenate([p_a.reshape(GROUPS, HB), pad], axis=1)
    rb = jnp.concatenate([p_b.reshape(GROUPS, HB), pad], axis=1)
    return jnp.stack([ra, rb], axis=1)


def _gdn_kernel(q_ref, k_ref, v_ref, z_ref, gc_ref, gr_ref, cwq_ref, cwk_ref, cwv_ref,
                pcol_ref, prow_ref, ng_ref, o_ref,
                xp_ref, qn_ref, kn_ref, vn_ref, bcol_ref, gcol_ref, grow_ref,
                val_ref, kq_ref, att_ref, kst_ref, *, seq):
    nch = seq // CHUNK
    grp = pl.program_id(1)
    tril, strict, triu = _tri_masks(CHUNK)
    tril16 = _ones_where(tril)
    triu16 = _ones_where(triu)

    alog_row = pcol_ref[0, 0:1, :]
    dtb_row = pcol_ref[0, 1:2, :]
    neg_decay_rate = -jnp.exp(alog_row)

    def gate_cols(c, carry):
        rows = pl.ds(pl.multiple_of(c * CHUNK, CHUNK), CHUNK)
        ga = gc_ref[rows, :]
        g = neg_decay_rate * _softplus(ga + dtb_row)
        gcol_ref[rows, :] = _cumsum_rows(g, tril16)
        bcol_ref[rows, :] = _sigmoid(ga)
        return carry

    lax.fori_loop(0, nch, gate_cols, 0, unroll=4)

    for i in range(HB):
        hh = grp * HB + i
        a_r = gr_ref[hh, 0]
        alog = jnp.full((nch, CHUNK), prow_ref[0, hh], F32)
        g_r = -jnp.exp(alog) * _softplus(a_r + prow_ref[1, hh])
        grow_ref[i] = _cumsum_lanes(g_r, triu16)

    tile = 256
    xp_ref[0:8, :] = jnp.zeros((8, LANES), F32)
    jobs = []
    for i in range(HB):
        cs = slice(i * LANES, (i + 1) * LANES)
        jobs.append((q_ref, cwq_ref, qn_ref, cs, HEAD_DV ** -0.5))
        jobs.append((k_ref, cwk_ref, kn_ref, cs, 1.0))
        jobs.append((v_ref, cwv_ref, vn_ref, cs, None))
    for src, cw_ref, dst, cs, scale in jobs:
        xp_ref[8:seq + 8, :] = src[:, cs]
        cw = cw_ref[:, cs]

        def conv_tile(t, carry, dst=dst, cs=cs, cw=cw, scale=scale):
            r0 = pl.multiple_of(t * tile, tile)
            y = cw[3:4, :] * xp_ref[pl.ds(r0 + 8, tile), :]
            for j in range(GDN_CONV - 1):
                y = y + cw[j:j + 1, :] * xp_ref[pl.ds(r0 + 5 + j, tile), :]
            y = y * _sigmoid(y)
            if scale is not None:
                y = y * lax.rsqrt(jnp.sum(y * y, axis=-1, keepdims=True) + NORM_EPS)
                if scale != 1.0:
                    y = y * scale
            dst[pl.ds(r0, tile), cs] = y
            return carry

        lax.fori_loop(0, seq // tile, conv_tile, 0, unroll=2)

    chunks_per_iter = 4

    def intra(cc, carry):
        items = []
        for u in range(chunks_per_iter):
            c = cc * chunks_per_iter + u
            rows = pl.ds(pl.multiple_of(c * CHUNK, CHUNK), CHUNK)
            for i in range(HB):
                items.append((c, rows, i, slice(i * LANES, (i + 1) * LANES)))
        q = [qn_ref[rows, cs] for _, rows, _, cs in items]
        k = [kn_ref[rows, cs] for _, rows, _, cs in items]
        v = [vn_ref[rows, cs] for _, rows, _, cs in items]
        beta = [bcol_ref[rows, HB + i:HB + i + 1] for _, rows, i, _ in items]
        gc = [gcol_ref[rows, i:i + 1] for _, rows, i, _ in items]
        gr = [grow_ref[i, pl.ds(c, 1), :] for c, _, i, _ in items]
        decay = [jnp.where(tril, jnp.exp(jnp.where(tril, a - b, 0.0)), 0.0) for a, b in zip(gc, gr)]
        kb = [a * b for a, b in zip(k, beta)]
        k16 = [a.astype(BF16) for a in k]
        g_kk = [_dot_nt(a.astype(BF16), b) for a, b in zip(kb, k16)]
        g_qk = [_dot_nt(a.astype(BF16), b) for a, b in zip(q, k16)]
        a_mat = [jnp.where(strict, a * d, 0.0) for a, d in zip(g_kk, decay)]
        inv_m1 = _unit_lower_inverse_minus_identity(a_mat)
        egc = [jnp.exp(a) for a in gc]
        rhs = [jnp.concatenate([a * b, c_ * e], axis=1)
               for a, b, c_, e in zip(v, beta, kb, egc)]
        sol = [a + _dot3(m, a) for m, a in zip(inv_m1, rhs)]
        for n, (c, rows, i, cs) in enumerate(items):
            val_ref[rows, cs] = sol[n][:, :LANES]
            kq_rows = pl.ds(pl.multiple_of(c * (2 * CHUNK), 2 * CHUNK), CHUNK)
            qi_rows = pl.ds(pl.multiple_of(c * (2 * CHUNK) + CHUNK, CHUNK), CHUNK)
            kq_ref[kq_rows, cs] = sol[n][:, LANES:].astype(BF16)
            kq_ref[qi_rows, cs] = (q[n] * egc[n]).astype(BF16)
            att_ref[i, rows, :] = jnp.where(tril, g_qk[n] * decay[n], 0.0).astype(BF16)
            gl = gc[n][CHUNK - 1:CHUNK, :]
            kst_ref[rows, cs] = (k[n] * jnp.exp(gl - gc[n])).astype(BF16)
        return carry

    lax.fori_loop(0, nch // chunks_per_iter, intra, 0)

    ng = ng_ref[...]

    def step(c, states):
        r0 = pl.multiple_of(c * CHUNK, CHUNK)
        rows = pl.ds(r0, CHUNK)
        heads = range(HB)
        cols = [slice(i * LANES, (i + 1) * LANES) for i in heads]
        kq_rows = pl.ds(pl.multiple_of(c * (2 * CHUNK), 2 * CHUNK), 2 * CHUNK)
        r = [_dot(kq_ref[kq_rows, cols[i]], states[i].astype(BF16)) for i in heads]
        vn16 = [(val_ref[rows, cols[i]] - r[i][:CHUNK]).astype(BF16) for i in heads]
        o_intra = [_dot(att_ref[i, rows, :], vn16[i]) for i in heads]
        upd = [_dot_tn(kst_ref[rows, cols[i]], vn16[i]) for i in heads]
        new_states = []
        for i in heads:
            egl = jnp.exp(gcol_ref[pl.ds(r0 + CHUNK - 1, 1), i:i + 1])
            new_states.append(states[i] * egl + upd[i])
            z = z_ref[rows, cols[i]]
            o = r[i][CHUNK:] + o_intra[i]
            o_ref[rows, cols[i]] = (_rms(o, ng) * (z * _sigmoid(z))).astype(o_ref.dtype)
        return tuple(new_states)

    init = tuple(jnp.zeros((HEAD_DV, HEAD_DV), F32) for _ in range(HB))
    lax.fori_loop(0, nch, step, init)


def _gdn_core(proj, ga, gat, conv_w, a_log, dt_bias, norm_g, *, batch, seq):
    m = batch * seq
    nch = seq // CHUNK
    wb = HB * LANES
    g_ = GROUPS
    gat4 = gat.reshape(2 * HEADS, batch, nch, CHUNK)
    pcol = _group_rows(a_log, dt_bias)
    prow = jnp.stack([a_log, dt_bias], axis=0)
    kernel = functools.partial(_gdn_kernel, seq=seq)
    return pl.pallas_call(
        kernel, grid=(batch, g_),
        in_specs=[
            pl.BlockSpec((seq, wb), lambda b, g: (b, g)),
            pl.BlockSpec((seq, wb), lambda b, g: (b, g_ + g)),
            pl.BlockSpec((seq, wb), lambda b, g: (b, 2 * g_ + g)),
            pl.BlockSpec((seq, wb), lambda b, g: (b, 3 * g_ + g)),
            pl.BlockSpec((seq, LANES), lambda b, g: (b, g)),
            pl.BlockSpec((2 * HEADS, 1, nch, CHUNK), lambda b, g: (0, b, 0, 0)),
            pl.BlockSpec((GDN_CONV, wb), lambda b, g: (0, g)),
            pl.BlockSpec((GDN_CONV, wb), lambda b, g: (0, g_ + g)),
            pl.BlockSpec((GDN_CONV, wb), lambda b, g: (0, 2 * g_ + g)),
            pl.BlockSpec((1, 2, LANES), lambda b, g: (g, 0, 0)),
            pl.BlockSpec(memory_space=pltpu.SMEM),
            pl.BlockSpec((1, LANES), lambda b, g: (0, 0)),
        ],
        out_specs=pl.BlockSpec((seq, wb), lambda b, g: (b, g)),
        out_shape=jax.ShapeDtypeStruct((m, HEADS * HEAD_DV), BF16),
        scratch_shapes=[
            pltpu.VMEM((seq + 8, LANES), F32),
            pltpu.VMEM((seq, wb), F32),
            pltpu.VMEM((seq, wb), F32),
            pltpu.VMEM((seq, wb), F32),
            pltpu.VMEM((seq, LANES), F32),
            pltpu.VMEM((seq, LANES), F32),
            pltpu.VMEM((HB, nch, CHUNK), F32),
            pltpu.VMEM((seq, wb), F32),
            pltpu.VMEM((2 * seq, wb), BF16),
            pltpu.VMEM((HB, seq, CHUNK), BF16),
            pltpu.VMEM((seq, wb), BF16),
        ],
        compiler_params=pltpu.CompilerParams(dimension_semantics=("parallel", "arbitrary"),
                                             vmem_limit_bytes=VMEM_LIMIT),
        name="gdn_core",
    )(proj, proj, proj, proj, ga, gat4, conv_w, conv_w, conv_w, pcol, prow,
      norm_g.reshape(1, HEAD_DV))


def _mlstm_kernel(q_ref, k_ref, v_ref, og_ref, gc_ref, gr_ref, pcol_ref, prow_ref, ng_ref,
                  o_ref, icol_ref, bcol_ref, drow_ref, c_ref, *, seq):
    nch = seq // CHUNK
    dk = HEAD_DV // 2
    grp = pl.program_id(1)
    tril, _, triu = _tri_masks(CHUNK)
    tril16 = _ones_where(tril)
    triu16 = _ones_where(triu)
    bias_row = pcol_ref[0, 0:1, :]

    def gate_cols(c, carry):
        rows = pl.ds(pl.multiple_of(c * CHUNK, CHUNK), CHUNK)
        pre = _soft_cap(gc_ref[rows, :] + bias_row)
        icol_ref[rows, :] = pre
        bcol_ref[rows, :] = _cumsum_rows(-_softplus(-pre), tril16)
        return carry

    lax.fori_loop(0, nch, gate_cols, 0, unroll=4)

    for i in range(HB):
        hh = grp * HB + i
        i_r = _soft_cap(gr_ref[hh, 0] + prow_ref[0, hh])
        f_r = _soft_cap(gr_ref[HEADS + hh, 0] + prow_ref[1, hh])
        drow_ref[i] = i_r - _cumsum_lanes(-_softplus(-f_r), triu16)

    lane = lax.broadcasted_iota(jnp.int32, (CHUNK, LANES), 1)
    ones_col = _ones_where(lane == 0)
    c_ref[...] = jnp.zeros(c_ref.shape, F32)

    chunks_per_iter = 4
    head_lanes = [(lane >= i * dk) & (lane < (i + 1) * dk) for i in range(HB)]
    cols = [slice(i * LANES, (i + 1) * LANES) for i in range(HB)]

    def step(cc, m_states):
        m_st = list(m_states)
        items = [(u, i) for u in range(chunks_per_iter) for i in range(HB)]
        chunk = [cc * chunks_per_iter + u for u in range(chunks_per_iter)]
        rows = [pl.ds(pl.multiple_of(c * CHUNK, CHUNK), CHUNK) for c in chunk]
        qp = [q_ref[r, :] for r in rows]
        kp = [k_ref[r, :] for r in rows]
        kp16 = [x.astype(BF16) for x in kp]
        bcum = [bcol_ref[rows[u], HB + i:HB + i + 1] for u, i in items]
        ig = [icol_ref[rows[u], i:i + 1] for u, i in items]
        dlog = [jnp.where(tril, b + drow_ref[i, pl.ds(chunk[u], 1), :], -jnp.inf)
                for b, (u, i) in zip(bcum, items)]
        m_intra = [jnp.max(d, axis=-1, keepdims=True) for d in dlog]
        q16 = [(jnp.where(head_lanes[i], qp[u], 0.0) * dk ** -0.5).astype(BF16) for u, i in items]
        s_raw = [_dot_nt(q, kp16[u]) for q, (u, i) in zip(q16, items)]
        g = [b[CHUNK - 1:CHUNK, :] for b in bcum]
        a = [g_ - b + i_ for g_, b, i_ in zip(g, bcum, ig)]
        a_max = [jnp.max(x, axis=0, keepdims=True) for x in a]
        m_out, w_inter, w_old, m_new = [], [], [], []
        for n, (u, i) in enumerate(items):
            m_inter = bcum[n] + m_st[i]
            m_out.append(jnp.maximum(m_inter, m_intra[n]))
            w_inter.append(jnp.exp(m_inter - m_out[n]))
            m_new.append(jnp.maximum(g[n] + m_st[i], a_max[n]))
            w_old.append(jnp.exp(g[n] + m_st[i] - m_new[n]))
            m_st[i] = m_new[n]
        vaug = [jnp.concatenate([v_ref[rows[u], cols[i]].astype(BF16), ones_col], axis=1)
                for u, i in items]
        s16 = [(s * jnp.exp(d - m)).astype(BF16) for s, d, m in zip(s_raw, dlog, m_out)]
        sv = [_dot(s, v) for s, v in zip(s16, vaug)]
        kw = [(jnp.where(head_lanes[i], kp[u], 0.0) * jnp.exp(a[n] - m_new[n])).astype(BF16)
              for n, (u, i) in enumerate(items)]
        kv = [_dot_tn(k_, v) for k_, v in zip(kw, vaug)]
        cst = [c_ref[i] for i in range(HB)]
        for n, (u, i) in enumerate(items):
            nd = w_inter[n] * _dot(q16[n], cst[i].astype(BF16)) + sv[n]
            den = nd[:, LANES:LANES + 1]
            h = nd[:, :LANES] * (1.0 / jnp.maximum(jnp.abs(den), jnp.exp(-m_out[n])))
            cst[i] = w_old[n] * cst[i] + kv[n]
            hn = _rms(h, ng_ref[:, cols[i]])
            o_ref[rows[u], cols[i]] = (_sigmoid(og_ref[rows[u], cols[i]]) * hn).astype(o_ref.dtype)
        for i in range(HB):
            c_ref[i] = cst[i]
        return tuple(m_st)

    lax.fori_loop(0, nch // chunks_per_iter, step,
                  tuple(jnp.zeros((1, 1), F32) for _ in range(HB)))


def _mlstm_core(proj, ga, gat, gate_b, norm_g, *, batch, seq):
    m = batch * seq
    nch = seq // CHUNK
    wb = HB * LANES
    g_ = GROUPS
    gat4 = gat.reshape(2 * HEADS, batch, nch, CHUNK)
    pad = jnp.zeros((GROUPS, LANES - 2 * HB), F32)
    pcol = jnp.concatenate([gate_b[0].reshape(GROUPS, HB), gate_b[1].reshape(GROUPS, HB), pad],
                           axis=1).reshape(GROUPS, 1, LANES)
    kernel = functools.partial(_mlstm_kernel, seq=seq)
    return pl.pallas_call(
        kernel, grid=(batch, g_),
        in_specs=[
            pl.BlockSpec((seq, LANES), lambda b, g: (b, g)),
            pl.BlockSpec((seq, LANES), lambda b, g: (b, g_ + g)),
            pl.BlockSpec((seq, wb), lambda b, g: (b, g_ + g)),
            pl.BlockSpec((seq, wb), lambda b, g: (b, 2 * g_ + g)),
            pl.BlockSpec((seq, LANES), lambda b, g: (b, g)),
            pl.BlockSpec((2 * HEADS, 1, nch, CHUNK), lambda b, g: (0, b, 0, 0)),
            pl.BlockSpec((1, 1, LANES), lambda b, g: (g, 0, 0)),
            pl.BlockSpec(memory_space=pltpu.SMEM),
            pl.BlockSpec((1, wb), lambda b, g: (0, g)),
        ],
        out_specs=pl.BlockSpec((seq, wb), lambda b, g: (b, g)),
        out_shape=jax.ShapeDtypeStruct((m, HEADS * HEAD_DV), BF16),
        scratch_shapes=[
            pltpu.VMEM((seq, LANES), F32),
            pltpu.VMEM((seq, LANES), F32),
            pltpu.VMEM((HB, nch, CHUNK), F32),
            pltpu.VMEM((HB, LANES, 2 * LANES), F32),
        ],
        compiler_params=pltpu.CompilerParams(dimension_semantics=("parallel", "arbitrary"),
                                             vmem_limit_bytes=VMEM_LIMIT),
        name="mlstm_core",
    )(proj, proj, proj, proj, ga, gat4, pcol, gate_b, norm_g.reshape(1, HEADS * HEAD_DV))


def _rope(x, cos, sin_lo, sin_hi):
    half = ROT_DIMS // 2
    return (x * cos + pltpu.roll(x, LANES - half, axis=1) * sin_lo
            + pltpu.roll(x, half, axis=1) * sin_hi)


def _diff_kernel(q_ref, k_ref, v_ref, tq_ref, tk_ref, lam_ref, ng_ref, o_ref,
                 kr_ref, vt_ref, q2_ref, m_ref, l_ref, acc_ref, *, seq, tq, dh, lambda_init):
    qi = pl.program_id(2)
    tk = tq

    @pl.when(qi == 0)
    def _():
        def prep_tile(t, carry):
            rows = pl.ds(pl.multiple_of(t * tk, tk), tk)
            for d in range(dh):
                cs = slice(d * LANES, (d + 1) * LANES)
                kr_ref[d, rows, :] = _rope(k_ref[rows, cs], tk_ref[0, rows, :], tk_ref[1, rows, :],
                                           tk_ref[2, rows, :]).astype(BF16)
                vt_ref[d, :, rows] = v_ref[rows, cs].T.astype(BF16)
            return carry
        lax.fori_loop(0, seq // tk, prep_tile, 0)

    lane = lax.broadcasted_iota(jnp.int32, (tq, LANES), 1)
    for d in range(dh):
        cs = slice(d * LANES, (d + 1) * LANES)
        q = _rope(q_ref[:, cs], tq_ref[0], tq_ref[1], tq_ref[2]) * DIFF_D ** -0.5
        q2_ref[d] = jnp.concatenate(
            [jnp.where(lane < DIFF_D, q, 0.0), jnp.where(lane >= DIFF_D, q, 0.0)],
            axis=0).astype(BF16)
    m_ref[...] = jnp.full(m_ref.shape, -jnp.inf, F32)
    l_ref[...] = jnp.zeros(l_ref.shape, F32)
    acc_ref[...] = jnp.zeros(acc_ref.shape, F32)

    def kv_block(j, masked):
        rows = pl.ds(pl.multiple_of(j * tk, tk), tk)
        st = [_dot_nt(kr_ref[d, rows, :], q2_ref[d]) for d in range(dh)]
        if masked:
            kpos = lax.broadcasted_iota(jnp.int32, (tk, 2 * tq), 0)
            qpos = lax.broadcasted_iota(jnp.int32, (tk, 2 * tq), 1)
            qpos = jnp.where(qpos >= tq, qpos - tq, qpos)
            shift = CHUNK.bit_length() - 1
            keep = jnp.right_shift(kpos, shift) <= jnp.right_shift(qpos, shift)
            st = [jnp.where(keep, s, -jnp.inf) for s in st]
        m_prev = [m_ref[d] for d in range(dh)]
        m_new = [jnp.maximum(m, jnp.max(s, axis=0, keepdims=True))
                 for m, s in zip(m_prev, st)]
        p = [jnp.exp(s - m) for s, m in zip(st, m_new)]
        pv = [_dot(vt_ref[d, :, rows], p[d].astype(BF16)) for d in range(dh)]
        for d in range(dh):
            alpha = jnp.exp(m_prev[d] - m_new[d])
            l_ref[d] = alpha * l_ref[d] + jnp.sum(p[d], axis=0, keepdims=True)
            acc_ref[d] = alpha * acc_ref[d] + pv[d]
            m_ref[d] = m_new[d]

    def body(j, carry):
        kv_block(j, False)
        return carry

    lax.fori_loop(0, qi, body, 0)
    kv_block(qi, True)

    lp = lam_ref[...]
    lam = (jnp.exp(jnp.sum(lp[0:1] * lp[1:2], axis=-1, keepdims=True))
           - jnp.exp(jnp.sum(lp[2:3] * lp[3:4], axis=-1, keepdims=True)) + lambda_init)
    for d in range(dh):
        out_t = acc_ref[d] * (1.0 / l_ref[d])
        o = (out_t[:, :tq] - lam * out_t[:, tq:]).T
        o_ref[:, d * LANES:(d + 1) * LANES] = (
            _rms(o, ng_ref[...]) * (1.0 - lambda_init)).astype(o_ref.dtype)


def _rope_tables(seq):
    half = ROT_DIMS // 2
    inv_freq = ROPE_THETA ** (-jnp.arange(0, ROT_DIMS, 2, dtype=F32) / ROT_DIMS)
    ang = jnp.arange(seq, dtype=F32)[:, None] * inv_freq[None, :]
    cos, sin = jnp.cos(ang), jnp.sin(ang)
    one = jnp.ones((seq, DIFF_D - ROT_DIMS), F32)
    zero = jnp.zeros((seq, DIFF_D - ROT_DIMS), F32)
    zh = jnp.zeros((seq, half), F32)
    cos_c = jnp.concatenate([cos, cos, one], axis=1)
    lo_c = jnp.concatenate([-sin, zh, zero], axis=1)
    hi_c = jnp.concatenate([zh, sin, zero], axis=1)
    rep = lambda t: jnp.concatenate([t, t], axis=1)
    return jnp.stack([rep(cos_c), rep(lo_c), rep(hi_c)], axis=0)


def _diff_core(proj, tables, lam_p, norm_g, lambda_init, *, batch, seq, tq=256, dh=2):
    m = batch * seq
    nq = seq // tq
    hg = HEADS // dh
    wb = dh * LANES
    kernel = functools.partial(_diff_kernel, seq=seq, tq=tq, dh=dh, lambda_init=lambda_init)
    return pl.pallas_call(
        kernel, grid=(batch, hg, nq),
        in_specs=[
            pl.BlockSpec((tq, wb), lambda b, h, i: (b * nq + i, h)),
            pl.BlockSpec((seq, wb), lambda b, h, i: (b, hg + h)),
            pl.BlockSpec((seq, wb), lambda b, h, i: (b, 2 * hg + h)),
            pl.BlockSpec((3, tq, LANES), lambda b, h, i: (0, i, 0)),
            pl.BlockSpec((3, seq, LANES), lambda b, h, i: (0, 0, 0)),
            pl.BlockSpec((4, DIFF_D), lambda b, h, i: (0, 0)),
            pl.BlockSpec((1, LANES), lambda b, h, i: (0, 0)),
        ],
        out_specs=pl.BlockSpec((tq, wb), lambda b, h, i: (b * nq + i, h)),
        out_shape=jax.ShapeDtypeStruct((m, HEADS * 2 * DIFF_D), BF16),
        scratch_shapes=[
            pltpu.VMEM((dh, seq, LANES), BF16),
            pltpu.VMEM((dh, LANES, seq), BF16),
            pltpu.VMEM((dh, 2 * tq, LANES), BF16),
            pltpu.VMEM((dh, 1, 2 * tq), F32),
            pltpu.VMEM((dh, 1, 2 * tq), F32),
            pltpu.VMEM((dh, LANES, 2 * tq), F32),
        ],
        compiler_params=pltpu.CompilerParams(
            dimension_semantics=("parallel", "parallel", "arbitrary"),
            vmem_limit_bytes=VMEM_LIMIT),
        name="diff_core",
    )(proj, proj, proj, tables, tables, lam_p, norm_g.reshape(1, 2 * DIFF_D))


def _out_proj_kernel(o_ref, w_ref, g_ref, x_ref, y_ref):
    y_ref[...] = x_ref[...] + _rms(_dot(o_ref[...], w_ref[...]), g_ref[...])


def _out_proj(o16, w16, g, x2, *, tm=1024):
    m, d = x2.shape
    return pl.pallas_call(
        _out_proj_kernel, grid=(m // tm,),
        in_specs=[pl.BlockSpec((tm, d), lambda i: (i, 0)),
                  pl.BlockSpec((d, d), lambda i: (0, 0)),
                  pl.BlockSpec((1, d), lambda i: (0, 0)),
                  pl.BlockSpec((tm, d), lambda i: (i, 0))],
        out_specs=pl.BlockSpec((tm, d), lambda i: (i, 0)),
        out_shape=jax.ShapeDtypeStruct((m, d), F32),
        compiler_params=pltpu.CompilerParams(dimension_semantics=("parallel",),
                                             vmem_limit_bytes=VMEM_LIMIT),
        name="out_proj",
    )(o16, w16, g, x2)


def _mlp_kernel(x_ref, g2_ref, w1_ref, w2_ref, g3_ref, y_ref, h_ref, acc_ref):
    j = pl.program_id(1)

    @pl.when(j == 0)
    def _():
        h_ref[...] = _rms(x_ref[...], g2_ref[...]).astype(BF16)
        acc_ref[...] = jnp.zeros(acc_ref.shape, F32)

    a = jnp.maximum(_dot(h_ref[...], w1_ref[...]), 0.0)
    acc_ref[...] += _dot((a * a).astype(BF16), w2_ref[...])

    @pl.when(j == pl.num_programs(1) - 1)
    def _():
        y_ref[...] = x_ref[...] + _rms(acc_ref[...], g3_ref[...])


def _mlp(x2, g2, w1_16, w2_16, g3, *, tm=1024, tf=1024):
    m, d = x2.shape
    f = w1_16.shape[1]
    return pl.pallas_call(
        _mlp_kernel, grid=(m // tm, f // tf),
        in_specs=[pl.BlockSpec((tm, d), lambda i, j: (i, 0)),
                  pl.BlockSpec((1, d), lambda i, j: (0, 0)),
                  pl.BlockSpec((d, tf), lambda i, j: (0, j)),
                  pl.BlockSpec((tf, d), lambda i, j: (j, 0)),
                  pl.BlockSpec((1, d), lambda i, j: (0, 0))],
        out_specs=pl.BlockSpec((tm, d), lambda i, j: (i, 0)),
        out_shape=jax.ShapeDtypeStruct((m, d), F32),
        scratch_shapes=[pltpu.VMEM((tm, d), BF16), pltpu.VMEM((tm, d), F32)],
        compiler_params=pltpu.CompilerParams(dimension_semantics=("parallel", "arbitrary"),
                                             vmem_limit_bytes=VMEM_LIMIT),
        name="mlp",
    )(x2, g2, w1_16, w2_16, g3)


def kernel(x, norm_g, mlp_w1, mlp_w2, gdn_w_in, gdn_conv, gdn_a_log, gdn_dt_bias, gdn_norm_g,
           gdn_w_out, mlstm_w_in, mlstm_gate_b, mlstm_norm_g, mlstm_w_out, diff_w_in,
           diff_lambda, diff_norm_g, diff_w_out):
    batch, seq, d = x.shape
    depth = norm_g.shape[0]
    x2 = x.reshape(batch * seq, d)
    tables = _rope_tables(seq)
    row = lambda v: v.reshape(1, -1)
    for i in range(depth):
        kind, j = i % 3, i // 3
        g_pre = row(norm_g[i, 0])
        if kind == 0:
            w_in = gdn_w_in[j]
            n_main = 4 * HEADS * HEAD_DV
            wc, wr = _gate_weights(w_in[:, n_main:n_main + HEADS], w_in[:, n_main + HEADS:])
            proj, ga, gat = _in_proj(x2, g_pre, w_in[:, :n_main].astype(BF16), wc, wr)
            o16 = _gdn_core(proj, ga, gat, gdn_conv[j], gdn_a_log[j], gdn_dt_bias[j],
                            gdn_norm_g[j], batch=batch, seq=seq)
            w_out = gdn_w_out[j]
        elif kind == 1:
            w_in = mlstm_w_in[j]
            n_main = 3 * HEADS * HEAD_DV
            wc, wr = _gate_weights(w_in[:, n_main:n_main + HEADS], w_in[:, n_main + HEADS:])
            proj, ga, gat = _in_proj(x2, g_pre, w_in[:, :n_main].astype(BF16), wc, wr)
            o16 = _mlstm_core(proj, ga, gat, mlstm_gate_b[j], mlstm_norm_g[j],
                              batch=batch, seq=seq)
            w_out = mlstm_w_out[j]
        else:
            lambda_init = 0.8 - 0.6 * math.exp(-0.3 * i)
            proj = _in_proj(x2, g_pre, diff_w_in[j].astype(BF16))
            o16 = _diff_core(proj, tables, diff_lambda[j], diff_norm_g[j], lambda_init,
                             batch=batch, seq=seq)
            w_out = diff_w_out[j]
        x2 = _out_proj(o16, w_out.astype(BF16), row(norm_g[i, 1]), x2)
        x2 = _mlp(x2, row(norm_g[i, 2]), mlp_w1[i].astype(BF16), mlp_w2[i].astype(BF16),
                  row(norm_g[i, 3]))
    return x2.reshape(batch, seq, d)
```

```python
import functools
import math

import jax
import jax.numpy as jnp
from jax import lax
from jax.experimental import pallas as pl
from jax.experimental.pallas import tpu as pltpu

F32 = jnp.float32
BF16 = jnp.bfloat16

D_MODEL = 1024
D_FF = 4 * D_MODEL
HEADS = 8
HEAD_DV = 128
CHUNK = 64
NORM_EPS = 1e-6
GATE_CAP = 15.0
GDN_CONV = 4
ROPE_THETA = 500000.0
DIFF_D = 64
ROT_DIMS = DIFF_D // 4
LANES = 128
HB = 2
GROUPS = HEADS // HB
VMEM_LIMIT = 56 * 1024 * 1024

_NT = (((1,), (1,)), ((), ()))
_TN = (((0,), (0,)), ((), ()))


def _dot(a, b):
    return jnp.dot(a, b, preferred_element_type=F32)


def _dot_nt(a, b):
    return lax.dot_general(a, b, _NT, preferred_element_type=F32)


def _dot_tn(a, b):
    return lax.dot_general(a, b, _TN, preferred_element_type=F32)


def _rms(x, g):
    ms = jnp.mean(x * x, axis=-1, keepdims=True)
    return x * lax.rsqrt(ms + NORM_EPS) * g


def _sigmoid(x):
    return 1.0 / (1.0 + jnp.exp(-x))


def _softplus(x):
    return jnp.maximum(x, 0.0) + jnp.log1p(jnp.exp(-jnp.abs(x)))


def _soft_cap(x):
    return GATE_CAP * jnp.tanh(x / GATE_CAP)


def _split3(x):
    x1 = x.astype(BF16)
    r1 = x - x1.astype(F32)
    x2 = r1.astype(BF16)
    r2 = r1 - x2.astype(F32)
    return x1, x2, r2.astype(BF16)


def _cumsum_rows(x, tril16):
    a, b, c = _split3(x)
    return _dot(tril16, a) + _dot(tril16, b) + _dot(tril16, c)


def _cumsum_lanes(x, triu16):
    a, b, c = _split3(x)
    return _dot(a, triu16) + _dot(b, triu16) + _dot(c, triu16)


def _split2(x):
    hi = x.astype(BF16)
    return hi, (x - hi.astype(F32)).astype(BF16)


def _dot3(a, b):
    ah, al = _split2(a)
    bh, bl = _split2(b)
    return _dot(ah, bh) + _dot(ah, bl) + _dot(al, bh)


def _unit_lower_inverse_minus_identity(a):
    n = a[0].shape[0]
    r = lax.broadcasted_iota(jnp.int32, (n, n), 0)
    c = lax.broadcasted_iota(jnp.int32, (n, n), 1)
    pair = jnp.right_shift(r, 1) == jnp.right_shift(c, 1)
    inv_m1 = [-jnp.where(pair, x, 0.0) for x in a]
    s = 2
    while s < n:
        sh = s.bit_length() - 1
        merge = ((jnp.right_shift(r, sh + 1) == jnp.right_shift(c, sh + 1))
                 & (jnp.right_shift(r, sh) != jnp.right_shift(c, sh)))
        off = [jnp.where(merge, x, 0.0) for x in a]
        t = [_dot3(o, m) for o, m in zip(off, inv_m1)]
        y = [o + x for o, x in zip(off, t)]
        t = [_dot3(m, x) for m, x in zip(inv_m1, y)]
        inv_m1 = [m - (x + z) for m, x, z in zip(inv_m1, y, t)]
        s *= 2
    return inv_m1


def _tri_masks(n):
    r = lax.broadcasted_iota(jnp.int32, (n, n), 0)
    c = lax.broadcasted_iota(jnp.int32, (n, n), 1)
    return r >= c, r > c, r <= c


def _ones_where(mask):
    return jnp.where(mask, 1.0, 0.0).astype(BF16)


def _in_proj_kernel(x_ref, g_ref, w_ref, o_ref, h_ref):
    @pl.when(pl.program_id(1) == 0)
    def _():
        h_ref[...] = _rms(x_ref[...], g_ref[...]).astype(BF16)

    o_ref[...] = _dot(h_ref[...], w_ref[...]).astype(o_ref.dtype)


def _in_proj_gates_kernel(x_ref, g_ref, w_ref, wc_ref, wr_ref, o_ref, oc_ref, or_ref, h_ref):
    @pl.when(pl.program_id(1) == 0)
    def _():
        h = _rms(x_ref[...], g_ref[...]).astype(BF16)
        h_ref[...] = h
        oc_ref[...] = _dot(h, wc_ref[...])
        or_ref[...] = _dot_nt(wr_ref[...], h)

    o_ref[...] = _dot(h_ref[...], w_ref[...]).astype(o_ref.dtype)


def _in_proj(x2, g, w16, wc16=None, wr16=None, *, tm=512):
    m, d = x2.shape
    n = w16.shape[1]
    tn = n
    grid = (m // tm, n // tn)
    x_spec = pl.BlockSpec((tm, d), lambda i, j: (i, 0))
    g_spec = pl.BlockSpec((1, d), lambda i, j: (0, 0))
    w_spec = pl.BlockSpec((d, tn), lambda i, j: (0, j))
    o_spec = pl.BlockSpec((tm, tn), lambda i, j: (i, j))
    params = pltpu.CompilerParams(dimension_semantics=("parallel", "arbitrary"),
                                  vmem_limit_bytes=VMEM_LIMIT)
    if wc16 is None:
        return pl.pallas_call(
            _in_proj_kernel, grid=grid,
            in_specs=[x_spec, g_spec, w_spec], out_specs=o_spec,
            out_shape=jax.ShapeDtypeStruct((m, n), F32),
            scratch_shapes=[pltpu.VMEM((tm, d), BF16)],
            compiler_params=params, name="in_proj",
        )(x2, g, w16)
    nc = wc16.shape[1]
    nr = wr16.shape[0]
    return pl.pallas_call(
        _in_proj_gates_kernel, grid=grid,
        in_specs=[x_spec, g_spec, w_spec,
                  pl.BlockSpec((d, nc), lambda i, j: (0, 0)),
                  pl.BlockSpec((nr, d), lambda i, j: (0, 0))],
        out_specs=[o_spec,
                   pl.BlockSpec((tm, nc), lambda i, j: (i, 0)),
                   pl.BlockSpec((nr, tm), lambda i, j: (0, i))],
        out_shape=[jax.ShapeDtypeStruct((m, n), F32),
                   jax.ShapeDtypeStruct((m, nc), F32),
                   jax.ShapeDtypeStruct((nr, m), F32)],
        scratch_shapes=[pltpu.VMEM((tm, d), BF16)],
        compiler_params=params, name="in_proj_gates",
    )(x2, g, w16, wc16, wr16)


def _gate_weights(w_a, w_b):
    d = w_a.shape[0]
    a = w_a.reshape(d, GROUPS, HB)
    b = w_b.reshape(d, GROUPS, HB)
    pad = jnp.zeros((d, GROUPS, LANES - 2 * HB), w_a.dtype)
    wc = jnp.concatenate([a, b, pad], axis=2).reshape(d, GROUPS * LANES)
    wr = jnp.concatenate([w_a, w_b], axis=1).T
    return wc.astype(BF16), wr.astype(BF16)


def _group_rows(p_a, p_b):
    pad = jnp.zeros((GROUPS, LANES - HB), F32)
    ra = jnp.concatenate([p_a.reshape(GROUPS, HB), pad], axis=1)
    rb = jnp.concatenate([p_b.reshape(GROUPS, HB), pad], axis=1)
    return jnp.stack([ra, rb], axis=1)


def _gdn_kernel(q_ref, k_ref, v_ref, z_ref, gc_ref, gr_ref, cwq_ref, cwk_ref, cwv_ref,
                pcol_ref, prow_ref, ng_ref, o_ref,
                xp_ref, qn_ref, kn_ref, vn_ref, bcol_ref, gcol_ref, grow_ref,
                val_ref, kq_ref, att_ref, kst_ref, *, seq):
    nch = seq // CHUNK
    grp = pl.program_id(1)
    tril, strict, triu = _tri_masks(CHUNK)
    tril16 = _ones_where(tril)
    triu16 = _ones_where(triu)

    alog_row = pcol_ref[0, 0:1, :]
    dtb_row = pcol_ref[0, 1:2, :]
    neg_decay_rate = -jnp.exp(alog_row)

    def gate_cols(c, carry):
        rows = pl.ds(pl.multiple_of(c * CHUNK, CHUNK), CHUNK)
        ga = gc_ref[rows, :]
        g = neg_decay_rate * _softplus(ga + dtb_row)
        gcol_ref[rows, :] = _cumsum_rows(g, tril16)
        bcol_ref[rows, :] = _sigmoid(ga)
        return carry

    lax.fori_loop(0, nch, gate_cols, 0, unroll=4)

    for i in range(HB):
        hh = grp * HB + i
        a_r = gr_ref[hh, 0]
        alog = jnp.full((nch, CHUNK), prow_ref[0, hh], F32)
        g_r = -jnp.exp(alog) * _softplus(a_r + prow_ref[1, hh])
        grow_ref[i] = _cumsum_lanes(g_r, triu16)

    tile = 256
    xp_ref[0:8, :] = jnp.zeros((8, LANES), F32)
    jobs = []
    for i in range(HB):
        cs = slice(i * LANES, (i + 1) * LANES)
        jobs.append((q_ref, cwq_ref, qn_ref, cs, HEAD_DV ** -0.5))
        jobs.append((k_ref, cwk_ref, kn_ref, cs, 1.0))
        jobs.append((v_ref, cwv_ref, vn_ref, cs, None))
    for src, cw_ref, dst, cs, scale in jobs:
        xp_ref[8:seq + 8, :] = src[:, cs]
        cw = cw_ref[:, cs]

        def conv_tile(t, carry, dst=dst, cs=cs, cw=cw, scale=scale):
            r0 = pl.multiple_of(t * tile, tile)
            y = cw[3:4, :] * xp_ref[pl.ds(r0 + 8, tile), :]
            for j in range(GDN_CONV - 1):
                y = y + cw[j:j + 1, :] * xp_ref[pl.ds(r0 + 5 + j, tile), :]
            y = y * _sigmoid(y)
            if scale is not None:
                y = y * lax.rsqrt(jnp.sum(y * y, axis=-1, keepdims=True) + NORM_EPS)
                if scale != 1.0:
                    y = y * scale
            dst[pl.ds(r0, tile), cs] = y
            return carry

        lax.fori_loop(0, seq // tile, conv_tile, 0, unroll=2)

    chunks_per_iter = 4

    def intra(cc, carry):
        items = []
        for u in range(chunks_per_iter):
            c = cc * chunks_per_iter + u
            rows = pl.ds(pl.multiple_of(c * CHUNK, CHUNK), CHUNK)
            for i in range(HB):
                items.append((c, rows, i, slice(i * LANES, (i + 1) * LANES)))
        q = [qn_ref[rows, cs] for _, rows, _, cs in items]
        k = [kn_ref[rows, cs] for _, rows, _, cs in items]
        v = [vn_ref[rows, cs] for _, rows, _, cs in items]
        beta = [bcol_ref[rows, HB + i:HB + i + 1] for _, rows, i, _ in items]
        gc = [gcol_ref[rows, i:i + 1] for _, rows, i, _ in items]
        gr = [grow_ref[i, pl.ds(c, 1), :] for c, _, i, _ in items]
        decay = [jnp.where(tril, jnp.exp(jnp.where(tril, a - b, 0.0)), 0.0) for a, b in zip(gc, gr)]
        kb = [a * b for a, b in zip(k, beta)]
        k16 = [a.astype(BF16) for a in k]
        g_kk = [_dot_nt(a.astype(BF16), b) for a, b in zip(kb, k16)]
        g_qk = [_dot_nt(a.astype(BF16), b) for a, b in zip(q, k16)]
        a_mat = [jnp.where(strict, a * d, 0.0) for a, d in zip(g_kk, decay)]
        inv_m1 = _unit_lower_inverse_minus_identity(a_mat)
        egc = [jnp.exp(a) for a in gc]
        rhs = [jnp.concatenate([a * b, c_ * e], axis=1)
               for a, b, c_, e in zip(v, beta, kb, egc)]
        sol = [a + _dot3(m, a) for m, a in zip(inv_m1, rhs)]
        for n, (c, rows, i, cs) in enumerate(items):
            val_ref[rows, cs] = sol[n][:, :LANES]
            kq_rows = pl.ds(pl.multiple_of(c * (2 * CHUNK), 2 * CHUNK), CHUNK)
            qi_rows = pl.ds(pl.multiple_of(c * (2 * CHUNK) + CHUNK, CHUNK), CHUNK)
            kq_ref[kq_rows, cs] = sol[n][:, LANES:].astype(BF16)
            kq_ref[qi_rows, cs] = (q[n] * egc[n]).astype(BF16)
            att_ref[i, rows, :] = jnp.where(tril, g_qk[n] * decay[n], 0.0).astype(BF16)
            gl = gc[n][CHUNK - 1:CHUNK, :]
            kst_ref[rows, cs] = (k[n] * jnp.exp(gl - gc[n])).astype(BF16)
        return carry

    lax.fori_loop(0, nch // chunks_per_iter, intra, 0)

    ng = ng_ref[...]

    def step(c, states):
        r0 = pl.multiple_of(c * CHUNK, CHUNK)
        rows = pl.ds(r0, CHUNK)
        heads = range(HB)
        cols = [slice(i * LANES, (i + 1) * LANES) for i in heads]
        kq_rows = pl.ds(pl.multiple_of(c * (2 * CHUNK), 2 * CHUNK), 2 * CHUNK)
        r = [_dot(kq_ref[kq_rows, cols[i]], states[i].astype(BF16)) for i in heads]
        vn16 = [(val_ref[rows, cols[i]] - r[i][:CHUNK]).astype(BF16) for i in heads]
        o_intra = [_dot(att_ref[i, rows, :], vn16[i]) for i in heads]
        upd = [_dot_tn(kst_ref[rows, cols[i]], vn16[i]) for i in heads]
        new_states = []
        for i in heads:
            egl = jnp.exp(gcol_ref[pl.ds(r0 + CHUNK - 1, 1), i:i + 1])
            new_states.append(states[i] * egl + upd[i])
            z = z_ref[rows, cols[i]]
            o = r[i][CHUNK:] + o_intra[i]
            o_ref[rows, cols[i]] = (_rms(o, ng) * (z * _sigmoid(z))).astype(o_ref.dtype)
        return tuple(new_states)

    init = tuple(jnp.zeros((HEAD_DV, HEAD_DV), F32) for _ in range(HB))
    lax.fori_loop(0, nch, step, init)


def _gdn_core(proj, ga, gat, conv_w, a_log, dt_bias, norm_g, *, batch, seq):
    m = batch * seq
    nch = seq // CHUNK
    wb = HB * LANES
    g_ = GROUPS
    gat4 = gat.reshape(2 * HEADS, batch, nch, CHUNK)
    pcol = _group_rows(a_log, dt_bias)
    prow = jnp.stack([a_log, dt_bias], axis=0)
    kernel = functools.partial(_gdn_kernel, seq=seq)
    return pl.pallas_call(
        kernel, grid=(batch, g_),
        in_specs=[
            pl.BlockSpec((seq, wb), lambda b, g: (b, g)),
            pl.BlockSpec((seq, wb), lambda b, g: (b, g_ + g)),
            pl.BlockSpec((seq, wb), lambda b, g: (b, 2 * g_ + g)),
            pl.BlockSpec((seq, wb), lambda b, g: (b, 3 * g_ + g)),
            pl.BlockSpec((seq, LANES), lambda b, g: (b, g)),
            pl.BlockSpec((2 * HEADS, 1, nch, CHUNK), lambda b, g: (0, b, 0, 0)),
            pl.BlockSpec((GDN_CONV, wb), lambda b, g: (0, g)),
            pl.BlockSpec((GDN_CONV, wb), lambda b, g: (0, g_ + g)),
            pl.BlockSpec((GDN_CONV, wb), lambda b, g: (0, 2 * g_ + g)),
            pl.BlockSpec((1, 2, LANES), lambda b, g: (g, 0, 0)),
            pl.BlockSpec(memory_space=pltpu.SMEM),
            pl.BlockSpec((1, LANES), lambda b, g: (0, 0)),
        ],
        out_specs=pl.BlockSpec((seq, wb), lambda b, g: (b, g)),
        out_shape=jax.ShapeDtypeStruct((m, HEADS * HEAD_DV), BF16),
        scratch_shapes=[
            pltpu.VMEM((seq + 8, LANES), F32),
            pltpu.VMEM((seq, wb), F32),
            pltpu.VMEM((seq, wb), F32),
            pltpu.VMEM((seq, wb), F32),
            pltpu.VMEM((seq, LANES), F32),
            pltpu.VMEM((seq, LANES), F32),
            pltpu.VMEM((HB, nch, CHUNK), F32),
            pltpu.VMEM((seq, wb), F32),
            pltpu.VMEM((2 * seq, wb), BF16),
            pltpu.VMEM((HB, seq, CHUNK), BF16),
            pltpu.VMEM((seq, wb), BF16),
        ],
        compiler_params=pltpu.CompilerParams(dimension_semantics=("parallel", "arbitrary"),
                                             vmem_limit_bytes=VMEM_LIMIT),
        name="gdn_core",
    )(proj, proj, proj, proj, ga, gat4, conv_w, conv_w, conv_w, pcol, prow,
      norm_g.reshape(1, HEAD_DV))


def _mlstm_kernel(q_ref, k_ref, v_ref, og_ref, gc_ref, gr_ref, pcol_ref, prow_ref, ng_ref,
                  o_ref, icol_ref, bcol_ref, drow_ref, c_ref, *, seq):
    nch = seq // CHUNK
    dk = HEAD_DV // 2
    grp = pl.program_id(1)
    tril, _, triu = _tri_masks(CHUNK)
    tril16 = _ones_where(tril)
    triu16 = _ones_where(triu)
    bias_row = pcol_ref[0, 0:1, :]

    def gate_cols(c, carry):
        rows = pl.ds(pl.multiple_of(c * CHUNK, CHUNK), CHUNK)
        pre = _soft_cap(gc_ref[rows, :] + bias_row)
        icol_ref[rows, :] = pre
        bcol_ref[rows, :] = _cumsum_rows(-_softplus(-pre), tril16)
        return carry

    lax.fori_loop(0, nch, gate_cols, 0, unroll=4)

    for i in range(HB):
        hh = grp * HB + i
        i_r = _soft_cap(gr_ref[hh, 0] + prow_ref[0, hh])
        f_r = _soft_cap(gr_ref[HEADS + hh, 0] + prow_ref[1, hh])
        drow_ref[i] = i_r - _cumsum_lanes(-_softplus(-f_r), triu16)

    lane = lax.broadcasted_iota(jnp.int32, (CHUNK, LANES), 1)
    ones_col = _ones_where(lane == 0)
    c_ref[...] = jnp.zeros(c_ref.shape, F32)

    chunks_per_iter = 4
    head_lanes = [(lane >= i * dk) & (lane < (i + 1) * dk) for i in range(HB)]
    cols = [slice(i * LANES, (i + 1) * LANES) for i in range(HB)]

    def step(cc, m_states):
        m_st = list(m_states)
        items = [(u, i) for u in range(chunks_per_iter) for i in range(HB)]
        chunk = [cc * chunks_per_iter + u for u in range(chunks_per_iter)]
        rows = [pl.ds(pl.multiple_of(c * CHUNK, CHUNK), CHUNK) for c in chunk]
        qp = [q_ref[r, :] for r in rows]
        kp = [k_ref[r, :] for r in rows]
        kp16 = [x.astype(BF16) for x in kp]
        bcum = [bcol_ref[rows[u], HB + i:HB + i + 1] for u, i in items]
        ig = [icol_ref[rows[u], i:i + 1] for u, i in items]
        dlog = [jnp.where(tril, b + drow_ref[i, pl.ds(chunk[u], 1), :], -jnp.inf)
                for b, (u, i) in zip(bcum, items)]
        m_intra = [jnp.max(d, axis=-1, keepdims=True) for d in dlog]
        q16 = [(jnp.where(head_lanes[i], qp[u], 0.0) * dk ** -0.5).astype(BF16) for u, i in items]
        s_raw = [_dot_nt(q, kp16[u]) for q, (u, i) in zip(q16, items)]
        g = [b[CHUNK - 1:CHUNK, :] for b in bcum]
        a = [g_ - b + i_ for g_, b, i_ in zip(g, bcum, ig)]
        a_max = [jnp.max(x, axis=0, keepdims=True) for x in a]
        m_out, w_inter, w_old, m_new = [], [], [], []
        for n, (u, i) in enumerate(items):
            m_inter = bcum[n] + m_st[i]
            m_out.append(jnp.maximum(m_inter, m_intra[n]))
            w_inter.append(jnp.exp(m_inter - m_out[n]))
            m_new.append(jnp.maximum(g[n] + m_st[i], a_max[n]))
            w_old.append(jnp.exp(g[n] + m_st[i] - m_new[n]))
            m_st[i] = m_new[n]
        vaug = [jnp.concatenate([v_ref[rows[u], cols[i]].astype(BF16), ones_col], axis=1)
                for u, i in items]
        s16 = [(s * jnp.exp(d - m)).astype(BF16) for s, d, m in zip(s_raw, dlog, m_out)]
        sv = [_dot(s, v) for s, v in zip(s16, vaug)]
        kw = [(jnp.where(head_lanes[i], kp[u], 0.0) * jnp.exp(a[n] - m_new[n])).astype(BF16)
              for n, (u, i) in enumerate(items)]
        kv = [_dot_tn(k_, v) for k_, v in zip(kw, vaug)]
        cst = [c_ref[i] for i in range(HB)]
        for n, (u, i) in enumerate(items):
            nd = w_inter[n] * _dot(q16[n], cst[i].astype(BF16)) + sv[n]
            den = nd[:, LANES:LANES + 1]
            h = nd[:, :LANES] * (1.0 / jnp.maximum(jnp.abs(den), jnp.exp(-m_out[n])))
            cst[i] = w_old[n] * cst[i] + kv[n]
            hn = _rms(h, ng_ref[:, cols[i]])
            o_ref[rows[u], cols[i]] = (_sigmoid(og_ref[rows[u], cols[i]]) * hn).astype(o_ref.dtype)
        for i in range(HB):
            c_ref[i] = cst[i]
        return tuple(m_st)

    lax.fori_loop(0, nch // chunks_per_iter, step,
                  tuple(jnp.zeros((1, 1), F32) for _ in range(HB)))


def _mlstm_core(proj, ga, gat, gate_b, norm_g, *, batch, seq):
    m = batch * seq
    nch = seq // CHUNK
    wb = HB * LANES
    g_ = GROUPS
    gat4 = gat.reshape(2 * HEADS, batch, nch, CHUNK)
    pad = jnp.zeros((GROUPS, LANES - 2 * HB), F32)
    pcol = jnp.concatenate([gate_b[0].reshape(GROUPS, HB), gate_b[1].reshape(GROUPS, HB), pad],
                           axis=1).reshape(GROUPS, 1, LANES)
    kernel = functools.partial(_mlstm_kernel, seq=seq)
    return pl.pallas_call(
        kernel, grid=(batch, g_),
        in_specs=[
            pl.BlockSpec((seq, LANES), lambda b, g: (b, g)),
            pl.BlockSpec((seq, LANES), lambda b, g: (b, g_ + g)),
            pl.BlockSpec((seq, wb), lambda b, g: (b, g_ + g)),
            pl.BlockSpec((seq, wb), lambda b, g: (b, 2 * g_ + g)),
            pl.BlockSpec((seq, LANES), lambda b, g: (b, g)),
            pl.BlockSpec((2 * HEADS, 1, nch, CHUNK), lambda b, g: (0, b, 0, 0)),
            pl.BlockSpec((1, 1, LANES), lambda b, g: (g, 0, 0)),
            pl.BlockSpec(memory_space=pltpu.SMEM),
            pl.BlockSpec((1, wb), lambda b, g: (0, g)),
        ],
        out_specs=pl.BlockSpec((seq, wb), lambda b, g: (b, g)),
        out_shape=jax.ShapeDtypeStruct((m, HEADS * HEAD_DV), BF16),
        scratch_shapes=[
            pltpu.VMEM((seq, LANES), F32),
            pltpu.VMEM((seq, LANES), F32),
            pltpu.VMEM((HB, nch, CHUNK), F32),
            pltpu.VMEM((HB, LANES, 2 * LANES), F32),
        ],
        compiler_params=pltpu.CompilerParams(dimension_semantics=("parallel", "arbitrary"),
                                             vmem_limit_bytes=VMEM_LIMIT),
        name="mlstm_core",
    )(proj, proj, proj, proj, ga, gat4, pcol, gate_b, norm_g.reshape(1, HEADS * HEAD_DV))


def _rope(x, cos, sin_lo, sin_hi):
    half = ROT_DIMS // 2
    return (x * cos + pltpu.roll(x, LANES - half, axis=1) * sin_lo
            + pltpu.roll(x, half, axis=1) * sin_hi)


def _diff_kernel(q_ref, k_ref, v_ref, tq_ref, tk_ref, lam_ref, ng_ref, o_ref,
                 kr_ref, vt_ref, q2_ref, m_ref, l_ref, acc_ref, *, seq, tq, dh, lambda_init):
    qi = pl.program_id(2)
    tk = tq

    @pl.when(qi == 0)
    def _():
        def prep_tile(t, carry):
            rows = pl.ds(pl.multiple_of(t * tk, tk), tk)
            for d in range(dh):
                cs = slice(d * LANES, (d + 1) * LANES)
                kr_ref[d, rows, :] = _rope(k_ref[rows, cs], tk_ref[0, rows, :], tk_ref[1, rows, :],
                                           tk_ref[2, rows, :]).astype(BF16)
                vt_ref[d, :, rows] = v_ref[rows, cs].T.astype(BF16)
            return carry
        lax.fori_loop(0, seq // tk, prep_tile, 0)

    lane = lax.broadcasted_iota(jnp.int32, (tq, LANES), 1)
    for d in range(dh):
        cs = slice(d * LANES, (d + 1) * LANES)
        q = _rope(q_ref[:, cs], tq_ref[0], tq_ref[1], tq_ref[2]) * DIFF_D ** -0.5
        q2_ref[d] = jnp.concatenate(
            [jnp.where(lane < DIFF_D, q, 0.0), jnp.where(lane >= DIFF_D, q, 0.0)],
            axis=0).astype(BF16)
    m_ref[...] = jnp.full(m_ref.shape, -jnp.inf, F32)
    l_ref[...] = jnp.zeros(l_ref.shape, F32)
    acc_ref[...] = jnp.zeros(acc_ref.shape, F32)

    def kv_block(j, masked):
        rows = pl.ds(pl.multiple_of(j * tk, tk), tk)
        st = [_dot_nt(kr_ref[d, rows, :], q2_ref[d]) for d in range(dh)]
        if masked:
            kpos = lax.broadcasted_iota(jnp.int32, (tk, 2 * tq), 0)
            qpos = lax.broadcasted_iota(jnp.int32, (tk, 2 * tq), 1)
            qpos = jnp.where(qpos >= tq, qpos - tq, qpos)
            shift = CHUNK.bit_length() - 1
            keep = jnp.right_shift(kpos, shift) <= jnp.right_shift(qpos, shift)
            st = [jnp.where(keep, s, -jnp.inf) for s in st]
        m_prev = [m_ref[d] for d in range(dh)]
        m_new = [jnp.maximum(m, jnp.max(s, axis=0, keepdims=True))
                 for m, s in zip(m_prev, st)]
        p = [jnp.exp(s - m) for s, m in zip(st, m_new)]
        pv = [_dot(vt_ref[d, :, rows], p[d].astype(BF16)) for d in range(dh)]
        for d in range(dh):
            alpha = jnp.exp(m_prev[d] - m_new[d])
            l_ref[d] = alpha * l_ref[d] + jnp.sum(p[d], axis=0, keepdims=True)
            acc_ref[d] = alpha * acc_ref[d] + pv[d]
            m_ref[d] = m_new[d]

    def body(j, carry):
        kv_block(j, False)
        return carry

    lax.fori_loop(0, qi, body, 0)
    kv_block(qi, True)

    lp = lam_ref[...]
    lam = (jnp.exp(jnp.sum(lp[0:1] * lp[1:2], axis=-1, keepdims=True))
           - jnp.exp(jnp.sum(lp[2:3] * lp[3:4], axis=-1, keepdims=True)) + lambda_init)
    for d in range(dh):
        out_t = acc_ref[d] * (1.0 / l_ref[d])
        o = (out_t[:, :tq] - lam * out_t[:, tq:]).T
        o_ref[:, d * LANES:(d + 1) * LANES] = (
            _rms(o, ng_ref[...]) * (1.0 - lambda_init)).astype(o_ref.dtype)


def _rope_tables(seq):
    half = ROT_DIMS // 2
    inv_freq = ROPE_THETA ** (-jnp.arange(0, ROT_DIMS, 2, dtype=F32) / ROT_DIMS)
    ang = jnp.arange(seq, dtype=F32)[:, None] * inv_freq[None, :]
    cos, sin = jnp.cos(ang), jnp.sin(ang)
    one = jnp.ones((seq, DIFF_D - ROT_DIMS), F32)
    zero = jnp.zeros((seq, DIFF_D - ROT_DIMS), F32)
    zh = jnp.zeros((seq, half), F32)
    cos_c = jnp.concatenate([cos, cos, one], axis=1)
    lo_c = jnp.concatenate([-sin, zh, zero], axis=1)
    hi_c = jnp.concatenate([zh, sin, zero], axis=1)
    rep = lambda t: jnp.concatenate([t, t], axis=1)
    return jnp.stack([rep(cos_c), rep(lo_c), rep(hi_c)], axis=0)


def _diff_core(proj, tables, lam_p, norm_g, lambda_init, *, batch, seq, tq=256, dh=4):
    m = batch * seq
    nq = seq // tq
    hg = HEADS // dh
    wb = dh * LANES
    kernel = functools.partial(_diff_kernel, seq=seq, tq=tq, dh=dh, lambda_init=lambda_init)
    return pl.pallas_call(
        kernel, grid=(batch, hg, nq),
        in_specs=[
            pl.BlockSpec((tq, wb), lambda b, h, i: (b * nq + i, h)),
            pl.BlockSpec((seq, wb), lambda b, h, i: (b, hg + h)),
            pl.BlockSpec((seq, wb), lambda b, h, i: (b, 2 * hg + h)),
            pl.BlockSpec((3, tq, LANES), lambda b, h, i: (0, i, 0)),
            pl.BlockSpec((3, seq, LANES), lambda b, h, i: (0, 0, 0)),
            pl.BlockSpec((4, DIFF_D), lambda b, h, i: (0, 0)),
            pl.BlockSpec((1, LANES), lambda b, h, i: (0, 0)),
        ],
        out_specs=pl.BlockSpec((tq, wb), lambda b, h, i: (b * nq + i, h)),
        out_shape=jax.ShapeDtypeStruct((m, HEADS * 2 * DIFF_D), BF16),
        scratch_shapes=[
            pltpu.VMEM((dh, seq, LANES), BF16),
            pltpu.VMEM((dh, LANES, seq), BF16),
            pltpu.VMEM((dh, 2 * tq, LANES), BF16),
            pltpu.VMEM((dh, 1, 2 * tq), F32),
            pltpu.VMEM((dh, 1, 2 * tq), F32),
            pltpu.VMEM((dh, LANES, 2 * tq), F32),
        ],
        compiler_params=pltpu.CompilerParams(
            dimension_semantics=("parallel", "parallel", "arbitrary"),
            vmem_limit_bytes=VMEM_LIMIT),
        name="diff_core",
    )(proj, proj, proj, tables, tables, lam_p, norm_g.reshape(1, 2 * DIFF_D))


def _out_proj_kernel(o_ref, w_ref, g_ref, x_ref, y_ref):
    y_ref[...] = x_ref[...] + _rms(_dot(o_ref[...], w_ref[...]), g_ref[...])


def _out_proj(o16, w16, g, x2, *, tm=1024):
    m, d = x2.shape
    return pl.pallas_call(
        _out_proj_kernel, grid=(m // tm,),
        in_specs=[pl.BlockSpec((tm, d), lambda i: (i, 0)),
                  pl.BlockSpec((d, d), lambda i: (0, 0)),
                  pl.BlockSpec((1, d), lambda i: (0, 0)),
                  pl.BlockSpec((tm, d), lambda i: (i, 0))],
        out_specs=pl.BlockSpec((tm, d), lambda i: (i, 0)),
        out_shape=jax.ShapeDtypeStruct((m, d), F32),
        compiler_params=pltpu.CompilerParams(dimension_semantics=("parallel",),
                                             vmem_limit_bytes=VMEM_LIMIT),
        name="out_proj",
    )(o16, w16, g, x2)


def _mlp_kernel(x_ref, g2_ref, w1_ref, w2_ref, g3_ref, y_ref, h_ref, acc_ref):
    j = pl.program_id(1)

    @pl.when(j == 0)
    def _():
        h_ref[...] = _rms(x_ref[...], g2_ref[...]).astype(BF16)
        acc_ref[...] = jnp.zeros(acc_ref.shape, F32)

    a = jnp.maximum(_dot(h_ref[...], w1_ref[...]), 0.0)
    acc_ref[...] += _dot((a * a).astype(BF16), w2_ref[...])

    @pl.when(j == pl.num_programs(1) - 1)
    def _():
        y_ref[...] = x_ref[...] + _rms(acc_ref[...], g3_ref[...])


def _mlp(x2, g2, w1_16, w2_16, g3, *, tm=1024, tf=1024):
    m, d = x2.shape
    f = w1_16.shape[1]
    return pl.pallas_call(
        _mlp_kernel, grid=(m // tm, f // tf),
        in_specs=[pl.BlockSpec((tm, d), lambda i, j: (i, 0)),
                  pl.BlockSpec((1, d), lambda i, j: (0, 0)),
                  pl.BlockSpec((d, tf), lambda i, j: (0, j)),
                  pl.BlockSpec((tf, d), lambda i, j: (j, 0)),
                  pl.BlockSpec((1, d), lambda i, j: (0, 0))],
        out_specs=pl.BlockSpec((tm, d), lambda i, j: (i, 0)),
        out_shape=jax.ShapeDtypeStruct((m, d), F32),
        scratch_shapes=[pltpu.VMEM((tm, d), BF16), pltpu.VMEM((tm, d), F32)],
        compiler_params=pltpu.CompilerParams(dimension_semantics=("parallel", "arbitrary"),
                                             vmem_limit_bytes=VMEM_LIMIT),
        name="mlp",
    )(x2, g2, w1_16, w2_16, g3)


def kernel(x, norm_g, mlp_w1, mlp_w2, gdn_w_in, gdn_conv, gdn_a_log, gdn_dt_bias, gdn_norm_g,
           gdn_w_out, mlstm_w_in, mlstm_gate_b, mlstm_norm_g, mlstm_w_out, diff_w_in,
           diff_lambda, diff_norm_g, diff_w_out):
    batch, seq, d = x.shape
    depth = norm_g.shape[0]
    x2 = x.reshape(batch * seq, d)
    tables = _rope_tables(seq)
    row = lambda v: v.reshape(1, -1)
    for i in range(depth):
        kind, j = i % 3, i // 3
        g_pre = row(norm_g[i, 0])
        if kind == 0:
            w_in = gdn_w_in[j]
            n_main = 4 * HEADS * HEAD_DV
            wc, wr = _gate_weights(w_in[:, n_main:n_main + HEADS], w_in[:, n_main + HEADS:])
            proj, ga, gat = _in_proj(x2, g_pre, w_in[:, :n_main].astype(BF16), wc, wr)
            o16 = _gdn_core(proj, ga, gat, gdn_conv[j], gdn_a_log[j], gdn_dt_bias[j],
                            gdn_norm_g[j], batch=batch, seq=seq)
            w_out = gdn_w_out[j]
        elif kind == 1:
            w_in = mlstm_w_in[j]
            n_main = 3 * HEADS * HEAD_DV
            wc, wr = _gate_weights(w_in[:, n_main:n_main + HEADS], w_in[:, n_main + HEADS:])
            proj, ga, gat = _in_proj(x2, g_pre, w_in[:, :n_main].astype(BF16), wc, wr)
            o16 = _mlstm_core(proj, ga, gat, mlstm_gate_b[j], mlstm_norm_g[j],
                              batch=batch, seq=seq)
            w_out = mlstm_w_out[j]
        else:
            lambda_init = 0.8 - 0.6 * math.exp(-0.3 * i)
            proj = _in_proj(x2, g_pre, diff_w_in[j].astype(BF16))
            o16 = _diff_core(proj, tables, diff_lambda[j], diff_norm_g[j], lambda_init,
                             batch=batch, seq=seq)
            w_out = diff_w_out[j]
        x2 = _out_proj(o16, w_out.astype(BF16), row(norm_g[i, 1]), x2)
        x2 = _mlp(x2, row(norm_g[i, 2]), mlp_w1[i].astype(BF16), mlp_w2[i].astype(BF16),
                  row(norm_g[i, 3]))
    return x2.reshape(batch, seq, d)
```

```python
import functools
import math

import jax
import jax.numpy as jnp
from jax import lax
from jax.experimental import pallas as pl
from jax.experimental.pallas import tpu as pltpu

F32 = jnp.float32
BF16 = jnp.bfloat16

D_MODEL = 1024
D_FF = 4 * D_MODEL
HEADS = 8
HEAD_DV = 128
CHUNK = 64
NORM_EPS = 1e-6
GATE_CAP = 15.0
GDN_CONV = 4
ROPE_THETA = 500000.0
DIFF_D = 64
ROT_DIMS = DIFF_D // 4
LANES = 128
HB = 2
GROUPS = HEADS // HB
VMEM_LIMIT = 56 * 1024 * 1024

_NT = (((1,), (1,)), ((), ()))
_TN = (((0,), (0,)), ((), ()))


def _dot(a, b):
    return jnp.dot(a, b, preferred_element_type=F32)


def _dot_nt(a, b):
    return lax.dot_general(a, b, _NT, preferred_element_type=F32)


def _dot_tn(a, b):
    return lax.dot_general(a, b, _TN, preferred_element_type=F32)


def _rms(x, g):
    ms = jnp.mean(x * x, axis=-1, keepdims=True)
    return x * lax.rsqrt(ms + NORM_EPS) * g


def _sigmoid(x):
    return 1.0 / (1.0 + jnp.exp(-x))


def _softplus(x):
    return jnp.maximum(x, 0.0) + jnp.log1p(jnp.exp(-jnp.abs(x)))


def _soft_cap(x):
    return GATE_CAP * jnp.tanh(x / GATE_CAP)


def _split3(x):
    x1 = x.astype(BF16)
    r1 = x - x1.astype(F32)
    x2 = r1.astype(BF16)
    r2 = r1 - x2.astype(F32)
    return x1, x2, r2.astype(BF16)


def _cumsum_rows(x, tril16):
    a, b, c = _split3(x)
    return _dot(tril16, a) + _dot(tril16, b) + _dot(tril16, c)


def _cumsum_lanes(x, triu16):
    a, b, c = _split3(x)
    return _dot(a, triu16) + _dot(b, triu16) + _dot(c, triu16)


def _split2(x):
    hi = x.astype(BF16)
    return hi, (x - hi.astype(F32)).astype(BF16)


def _dot3(a, b):
    ah, al = _split2(a)
    bh, bl = _split2(b)
    return _dot(ah, bh) + _dot(ah, bl) + _dot(al, bh)


def _dot2(a, b):
    ah = a.astype(BF16)
    bh, bl = _split2(b)
    return _dot(ah, bh) + _dot(ah, bl)


def _unit_lower_inverse_minus_identity(a):
    n = a[0].shape[0]
    r = lax.broadcasted_iota(jnp.int32, (n, n), 0)
    c = lax.broadcasted_iota(jnp.int32, (n, n), 1)
    pair = jnp.right_shift(r, 1) == jnp.right_shift(c, 1)
    a = [x.astype(BF16).astype(F32) for x in a]
    inv_m1 = [-jnp.where(pair, x, 0.0) for x in a]
    s = 2
    while s < n:
        sh = s.bit_length() - 1
        merge = ((jnp.right_shift(r, sh + 1) == jnp.right_shift(c, sh + 1))
                 & (jnp.right_shift(r, sh) != jnp.right_shift(c, sh)))
        off = [jnp.where(merge, x, 0.0) for x in a]
        t = [_dot2(o, m) for o, m in zip(off, inv_m1)]
        y = [o + x for o, x in zip(off, t)]
        t = [_dot3(m, x) for m, x in zip(inv_m1, y)]
        inv_m1 = [m - (x + z) for m, x, z in zip(inv_m1, y, t)]
        s *= 2
    return inv_m1


def _tri_masks(n):
    r = lax.broadcasted_iota(jnp.int32, (n, n), 0)
    c = lax.broadcasted_iota(jnp.int32, (n, n), 1)
    return r >= c, r > c, r <= c


def _ones_where(mask):
    return jnp.where(mask, 1.0, 0.0).astype(BF16)


def _in_proj_kernel(x_ref, g_ref, w_ref, o_ref, h_ref):
    @pl.when(pl.program_id(1) == 0)
    def _():
        h_ref[...] = _rms(x_ref[...], g_ref[...]).astype(BF16)

    o_ref[...] = _dot(h_ref[...], w_ref[...]).astype(o_ref.dtype)


def _in_proj_gates_kernel(x_ref, g_ref, w_ref, wc_ref, wr_ref, o_ref, oc_ref, or_ref, h_ref):
    @pl.when(pl.program_id(1) == 0)
    def _():
        h = _rms(x_ref[...], g_ref[...]).astype(BF16)
        h_ref[...] = h
        oc_ref[...] = _dot(h, wc_ref[...])
        or_ref[...] = _dot_nt(wr_ref[...], h)

    o_ref[...] = _dot(h_ref[...], w_ref[...]).astype(o_ref.dtype)


def _in_proj(x2, g, w16, wc16=None, wr16=None, *, tm=512):
    m, d = x2.shape
    n = w16.shape[1]
    tn = n
    grid = (m // tm, n // tn)
    x_spec = pl.BlockSpec((tm, d), lambda i, j: (i, 0))
    g_spec = pl.BlockSpec((1, d), lambda i, j: (0, 0))
    w_spec = pl.BlockSpec((d, tn), lambda i, j: (0, j))
    o_spec = pl.BlockSpec((tm, tn), lambda i, j: (i, j))
    params = pltpu.CompilerParams(dimension_semantics=("parallel", "arbitrary"),
                                  vmem_limit_bytes=VMEM_LIMIT)
    if wc16 is None:
        return pl.pallas_call(
            _in_proj_kernel, grid=grid,
            in_specs=[x_spec, g_spec, w_spec], out_specs=o_spec,
            out_shape=jax.ShapeDtypeStruct((m, n), F32),
            scratch_shapes=[pltpu.VMEM((tm, d), BF16)],
            compiler_params=params, name="in_proj",
        )(x2, g, w16)
    nc = wc16.shape[1]
    nr = wr16.shape[0]
    return pl.pallas_call(
        _in_proj_gates_kernel, grid=grid,
        in_specs=[x_spec, g_spec, w_spec,
                  pl.BlockSpec((d, nc), lambda i, j: (0, 0)),
                  pl.BlockSpec((nr, d), lambda i, j: (0, 0))],
        out_specs=[o_spec,
                   pl.BlockSpec((tm, nc), lambda i, j: (i, 0)),
                   pl.BlockSpec((nr, tm), lambda i, j: (0, i))],
        out_shape=[jax.ShapeDtypeStruct((m, n), F32),
                   jax.ShapeDtypeStruct((m, nc), F32),
                   jax.ShapeDtypeStruct((nr, m), F32)],
        scratch_shapes=[pltpu.VMEM((tm, d), BF16)],
        compiler_params=params, name="in_proj_gates",
    )(x2, g, w16, wc16, wr16)


def _gate_weights(w_a, w_b):
    d = w_a.shape[0]
    a = w_a.reshape(d, GROUPS, HB)
    b = w_b.reshape(d, GROUPS, HB)
    pad = jnp.zeros((d, GROUPS, LANES - 2 * HB), w_a.dtype)
    wc = jnp.concatenate([a, b, pad], axis=2).reshape(d, GROUPS * LANES)
    wr = jnp.concatenate([w_a, w_b], axis=1).T
    return wc.astype(BF16), wr.astype(BF16)


def _group_rows(p_a, p_b):
    pad = jnp.zeros((GROUPS, LANES - HB), F32)
    ra = jnp.concatenate([p_a.reshape(GROUPS, HB), pad], axis=1)
    rb = jnp.concatenate([p_b.reshape(GROUPS, HB), pad], axis=1)
    return jnp.stack([ra, rb], axis=1)


def _gdn_kernel(q_ref, k_ref, v_ref, z_ref, gc_ref, gr_ref, cwq_ref, cwk_ref, cwv_ref,
                pcol_ref, prow_ref, ng_ref, o_ref,
                xp_ref, qn_ref, kn_ref, vn_ref, bcol_ref, gcol_ref, grow_ref,
                fold_ref, sadd_ref, oadd_ref, *, seq):
    nch = seq // CHUNK
    fold_rows = HEAD_DV + CHUNK
    grp = pl.program_id(1)
    tril, strict, triu = _tri_masks(CHUNK)
    tril16 = _ones_where(tril)
    triu16 = _ones_where(triu)

    alog_row = pcol_ref[0, 0:1, :]
    dtb_row = pcol_ref[0, 1:2, :]
    neg_decay_rate = -jnp.exp(alog_row)

    def gate_cols(c, carry):
        rows = pl.ds(pl.multiple_of(c * CHUNK, CHUNK), CHUNK)
        ga = gc_ref[rows, :]
        g = neg_decay_rate * _softplus(ga + dtb_row)
        gcol_ref[rows, :] = _cumsum_rows(g, tril16)
        bcol_ref[rows, :] = _sigmoid(ga)
        return carry

    lax.fori_loop(0, nch, gate_cols, 0, unroll=4)

    for i in range(HB):
        hh = grp * HB + i
        a_r = gr_ref[hh, 0]
        alog = jnp.full((nch, CHUNK), prow_ref[0, hh], F32)
        g_r = -jnp.exp(alog) * _softplus(a_r + prow_ref[1, hh])
        grow_ref[i] = _cumsum_lanes(g_r, triu16)

    tile = 256
    xp_ref[0:8, :] = jnp.zeros((8, LANES), F32)
    jobs = []
    for i in range(HB):
        cs = slice(i * LANES, (i + 1) * LANES)
        jobs.append((q_ref, cwq_ref, qn_ref, cs, HEAD_DV ** -0.5))
        jobs.append((k_ref, cwk_ref, kn_ref, cs, 1.0))
        jobs.append((v_ref, cwv_ref, vn_ref, cs, None))
    for src, cw_ref, dst, cs, scale in jobs:
        xp_ref[8:seq + 8, :] = src[:, cs]
        cw = cw_ref[:, cs]

        def conv_tile(t, carry, dst=dst, cs=cs, cw=cw, scale=scale):
            r0 = pl.multiple_of(t * tile, tile)
            y = cw[3:4, :] * xp_ref[pl.ds(r0 + 8, tile), :]
            for j in range(GDN_CONV - 1):
                y = y + cw[j:j + 1, :] * xp_ref[pl.ds(r0 + 5 + j, tile), :]
            y = y * _sigmoid(y)
            if scale is not None:
                y = y * lax.rsqrt(jnp.sum(y * y, axis=-1, keepdims=True) + NORM_EPS)
                if scale != 1.0:
                    y = y * scale
            dst[pl.ds(r0, tile), cs] = y
            return carry

        lax.fori_loop(0, seq // tile, conv_tile, 0, unroll=2)

    chunks_per_iter = 4

    def intra(cc, carry):
        items = []
        for u in range(chunks_per_iter):
            c = cc * chunks_per_iter + u
            rows = pl.ds(pl.multiple_of(c * CHUNK, CHUNK), CHUNK)
            for i in range(HB):
                items.append((c, rows, i, slice(i * LANES, (i + 1) * LANES)))
        q = [qn_ref[rows, cs] for _, rows, _, cs in items]
        k = [kn_ref[rows, cs] for _, rows, _, cs in items]
        v = [vn_ref[rows, cs] for _, rows, _, cs in items]
        beta = [bcol_ref[rows, HB + i:HB + i + 1] for _, rows, i, _ in items]
        gc = [gcol_ref[rows, i:i + 1] for _, rows, i, _ in items]
        gr = [grow_ref[i, pl.ds(c, 1), :] for c, _, i, _ in items]
        decay = [jnp.where(tril, jnp.exp(jnp.where(tril, a - b, 0.0)), 0.0) for a, b in zip(gc, gr)]
        kb = [a * b for a, b in zip(k, beta)]
        k16 = [a.astype(BF16) for a in k]
        g_kk = [_dot_nt(a.astype(BF16), b) for a, b in zip(kb, k16)]
        g_qk = [_dot_nt(a.astype(BF16), b) for a, b in zip(q, k16)]
        a_mat = [jnp.where(strict, a * d, 0.0) for a, d in zip(g_kk, decay)]
        inv_m1 = _unit_lower_inverse_minus_identity(a_mat)
        egc = [jnp.exp(a) for a in gc]
        rhs = [jnp.concatenate([a * b, c_ * e], axis=1)
               for a, b, c_, e in zip(v, beta, kb, egc)]
        sol = [a + _dot3(m, a) for m, a in zip(inv_m1, rhs)]
        sol16 = [a.astype(BF16) for a in sol]
        kst16 = [(a * jnp.exp(g_[CHUNK - 1:CHUNK, :] - g_)).astype(BF16) for a, g_ in zip(k, gc)]
        att16 = [jnp.where(tril, a * d, 0.0).astype(BF16) for a, d in zip(g_qk, decay)]
        kst_sol = [_dot_tn(a, b) for a, b in zip(kst16, sol16)]
        att_sol = [_dot(a, b) for a, b in zip(att16, sol16)]
        for n, (c, rows, i, cs) in enumerate(items):
            t_rows = pl.ds(pl.multiple_of(c * fold_rows, CHUNK), HEAD_DV)
            q_rows = pl.ds(pl.multiple_of(c * fold_rows + HEAD_DV, CHUNK), CHUNK)
            fold_ref[t_rows, cs] = (-kst_sol[n][:, LANES:]).astype(BF16)
            fold_ref[q_rows, cs] = (q[n] * egc[n] - att_sol[n][:, LANES:]).astype(BF16)
            sadd_ref[pl.ds(pl.multiple_of(c * HEAD_DV, HEAD_DV), HEAD_DV), cs] = kst_sol[n][:, :LANES]
            oadd_ref[rows, cs] = att_sol[n][:, :LANES]
        return carry

    lax.fori_loop(0, nch // chunks_per_iter, intra, 0)

    ng = ng_ref[...]

    def step(c, states):
        r0 = pl.multiple_of(c * CHUNK, CHUNK)
        rows = pl.ds(r0, CHUNK)
        heads = range(HB)
        cols = [slice(i * LANES, (i + 1) * LANES) for i in heads]
        f_rows = pl.ds(pl.multiple_of(c * fold_rows, CHUNK), fold_rows)
        s_rows = pl.ds(pl.multiple_of(c * HEAD_DV, HEAD_DV), HEAD_DV)
        r = [_dot(fold_ref[f_rows, cols[i]], states[i].astype(BF16)) for i in heads]
        new_states = []
        for i in heads:
            egl = jnp.exp(gcol_ref[pl.ds(r0 + CHUNK - 1, 1), i:i + 1])
            new_states.append(states[i] * egl + r[i][:HEAD_DV] + sadd_ref[s_rows, cols[i]])
            z = z_ref[rows, cols[i]]
            o = r[i][HEAD_DV:] + oadd_ref[rows, cols[i]]
            o_ref[rows, cols[i]] = (_rms(o, ng) * (z * _sigmoid(z))).astype(o_ref.dtype)
        return tuple(new_states)

    init = tuple(jnp.zeros((HEAD_DV, HEAD_DV), F32) for _ in range(HB))
    lax.fori_loop(0, nch, step, init)


def _gdn_core(proj, ga, gat, conv_w, a_log, dt_bias, norm_g, *, batch, seq):
    m = batch * seq
    nch = seq // CHUNK
    wb = HB * LANES
    g_ = GROUPS
    gat4 = gat.reshape(2 * HEADS, batch, nch, CHUNK)
    pcol = _group_rows(a_log, dt_bias)
    prow = jnp.stack([a_log, dt_bias], axis=0)
    kernel = functools.partial(_gdn_kernel, seq=seq)
    return pl.pallas_call(
        kernel, grid=(batch, g_),
        in_specs=[
            pl.BlockSpec((seq, wb), lambda b, g: (b, g)),
            pl.BlockSpec((seq, wb), lambda b, g: (b, g_ + g)),
            pl.BlockSpec((seq, wb), lambda b, g: (b, 2 * g_ + g)),
            pl.BlockSpec((seq, wb), lambda b, g: (b, 3 * g_ + g)),
            pl.BlockSpec((seq, LANES), lambda b, g: (b, g)),
            pl.BlockSpec((2 * HEADS, 1, nch, CHUNK), lambda b, g: (0, b, 0, 0)),
            pl.BlockSpec((GDN_CONV, wb), lambda b, g: (0, g)),
            pl.BlockSpec((GDN_CONV, wb), lambda b, g: (0, g_ + g)),
            pl.BlockSpec((GDN_CONV, wb), lambda b, g: (0, 2 * g_ + g)),
            pl.BlockSpec((1, 2, LANES), lambda b, g: (g, 0, 0)),
            pl.BlockSpec(memory_space=pltpu.SMEM),
            pl.BlockSpec((1, LANES), lambda b, g: (0, 0)),
        ],
        out_specs=pl.BlockSpec((seq, wb), lambda b, g: (b, g)),
        out_shape=jax.ShapeDtypeStruct((m, HEADS * HEAD_DV), BF16),
        scratch_shapes=[
            pltpu.VMEM((seq + 8, LANES), F32),
            pltpu.VMEM((seq, wb), F32),
            pltpu.VMEM((seq, wb), F32),
            pltpu.VMEM((seq, wb), F32),
            pltpu.VMEM((seq, LANES), F32),
            pltpu.VMEM((seq, LANES), F32),
            pltpu.VMEM((HB, nch, CHUNK), F32),
            pltpu.VMEM((nch * (HEAD_DV + CHUNK), wb), BF16),
            pltpu.VMEM((nch * HEAD_DV, wb), F32),
            pltpu.VMEM((seq, wb), F32),
        ],
        compiler_params=pltpu.CompilerParams(dimension_semantics=("parallel", "arbitrary"),
                                             vmem_limit_bytes=VMEM_LIMIT),
        name="gdn_core",
    )(proj, proj, proj, proj, ga, gat4, conv_w, conv_w, conv_w, pcol, prow,
      norm_g.reshape(1, HEAD_DV))


def _mlstm_kernel(q_ref, k_ref, v_ref, og_ref, gc_ref, gr_ref, pcol_ref, prow_ref, ng_ref,
                  o_ref, icol_ref, bcol_ref, drow_ref, c_ref, *, seq):
    nch = seq // CHUNK
    dk = HEAD_DV // 2
    grp = pl.program_id(1)
    tril, _, triu = _tri_masks(CHUNK)
    tril16 = _ones_where(tril)
    triu16 = _ones_where(triu)
    bias_row = pcol_ref[0, 0:1, :]

    def gate_cols(c, carry):
        rows = pl.ds(pl.multiple_of(c * CHUNK, CHUNK), CHUNK)
        pre = _soft_cap(gc_ref[rows, :] + bias_row)
        icol_ref[rows, :] = pre
        bcol_ref[rows, :] = _cumsum_rows(-_softplus(-pre), tril16)
        return carry

    lax.fori_loop(0, nch, gate_cols, 0, unroll=4)

    for i in range(HB):
        hh = grp * HB + i
        i_r = _soft_cap(gr_ref[hh, 0] + prow_ref[0, hh])
        f_r = _soft_cap(gr_ref[HEADS + hh, 0] + prow_ref[1, hh])
        drow_ref[i] = i_r - _cumsum_lanes(-_softplus(-f_r), triu16)

    lane = lax.broadcasted_iota(jnp.int32, (CHUNK, LANES), 1)
    ones_col = _ones_where(lane == 0)
    c_ref[...] = jnp.zeros(c_ref.shape, F32)

    chunks_per_iter = 4
    head_lanes = [(lane >= i * dk) & (lane < (i + 1) * dk) for i in range(HB)]
    cols = [slice(i * LANES, (i + 1) * LANES) for i in range(HB)]

    def step(cc, m_states):
        m_st = list(m_states)
        items = [(u, i) for u in range(chunks_per_iter) for i in range(HB)]
        chunk = [cc * chunks_per_iter + u for u in range(chunks_per_iter)]
        rows = [pl.ds(pl.multiple_of(c * CHUNK, CHUNK), CHUNK) for c in chunk]
        qp = [q_ref[r, :] for r in rows]
        kp = [k_ref[r, :] for r in rows]
        kp16 = [x.astype(BF16) for x in kp]
        bcum = [bcol_ref[rows[u], HB + i:HB + i + 1] for u, i in items]
        ig = [icol_ref[rows[u], i:i + 1] for u, i in items]
        dlog = [jnp.where(tril, b + drow_ref[i, pl.ds(chunk[u], 1), :], -jnp.inf)
                for b, (u, i) in zip(bcum, items)]
        m_intra = [jnp.max(d, axis=-1, keepdims=True) for d in dlog]
        q16 = [(jnp.where(head_lanes[i], qp[u], 0.0) * dk ** -0.5).astype(BF16) for u, i in items]
        s_raw = [_dot_nt(q, kp16[u]) for q, (u, i) in zip(q16, items)]
        g = [b[CHUNK - 1:CHUNK, :] for b in bcum]
        a = [g_ - b + i_ for g_, b, i_ in zip(g, bcum, ig)]
        a_max = [jnp.max(x, axis=0, keepdims=True) for x in a]
        m_out, w_inter, w_old, m_new = [], [], [], []
        for n, (u, i) in enumerate(items):
            m_inter = bcum[n] + m_st[i]
            m_out.append(jnp.maximum(m_inter, m_intra[n]))
            w_inter.append(jnp.exp(m_inter - m_out[n]))
            m_new.append(jnp.maximum(g[n] + m_st[i], a_max[n]))
            w_old.append(jnp.exp(g[n] + m_st[i] - m_new[n]))
            m_st[i] = m_new[n]
        vaug = [jnp.concatenate([v_ref[rows[u], cols[i]].astype(BF16), ones_col], axis=1)
                for u, i in items]
        s16 = [(s * jnp.exp(d - m)).astype(BF16) for s, d, m in zip(s_raw, dlog, m_out)]
        sv = [_dot(s, v) for s, v in zip(s16, vaug)]
        kw = [(jnp.where(head_lanes[i], kp[u], 0.0) * jnp.exp(a[n] - m_new[n])).astype(BF16)
              for n, (u, i) in enumerate(items)]
        kv = [_dot_tn(k_, v) for k_, v in zip(kw, vaug)]
        cst = [c_ref[i] for i in range(HB)]
        for n, (u, i) in enumerate(items):
            nd = w_inter[n] * _dot(q16[n], cst[i].astype(BF16)) + sv[n]
            den = nd[:, LANES:LANES + 1]
            h = nd[:, :LANES] * (1.0 / jnp.maximum(jnp.abs(den), jnp.exp(-m_out[n])))
            cst[i] = w_old[n] * cst[i] + kv[n]
            hn = _rms(h, ng_ref[:, cols[i]])
            o_ref[rows[u], cols[i]] = (_sigmoid(og_ref[rows[u], cols[i]]) * hn).astype(o_ref.dtype)
        for i in range(HB):
            c_ref[i] = cst[i]
        return tuple(m_st)

    lax.fori_loop(0, nch // chunks_per_iter, step,
                  tuple(jnp.zeros((1, 1), F32) for _ in range(HB)))


def _mlstm_core(proj, ga, gat, gate_b, norm_g, *, batch, seq):
    m = batch * seq
    nch = seq // CHUNK
    wb = HB * LANES
    g_ = GROUPS
    gat4 = gat.reshape(2 * HEADS, batch, nch, CHUNK)
    pad = jnp.zeros((GROUPS, LANES - 2 * HB), F32)
    pcol = jnp.concatenate([gate_b[0].reshape(GROUPS, HB), gate_b[1].reshape(GROUPS, HB), pad],
                           axis=1).reshape(GROUPS, 1, LANES)
    kernel = functools.partial(_mlstm_kernel, seq=seq)
    return pl.pallas_call(
        kernel, grid=(batch, g_),
        in_specs=[
            pl.BlockSpec((seq, LANES), lambda b, g: (b, g)),
            pl.BlockSpec((seq, LANES), lambda b, g: (b, g_ + g)),
            pl.BlockSpec((seq, wb), lambda b, g: (b, g_ + g)),
            pl.BlockSpec((seq, wb), lambda b, g: (b, 2 * g_ + g)),
            pl.BlockSpec((seq, LANES), lambda b, g: (b, g)),
            pl.BlockSpec((2 * HEADS, 1, nch, CHUNK), lambda b, g: (0, b, 0, 0)),
            pl.BlockSpec((1, 1, LANES), lambda b, g: (g, 0, 0)),
            pl.BlockSpec(memory_space=pltpu.SMEM),
            pl.BlockSpec((1, wb), lambda b, g: (0, g)),
        ],
        out_specs=pl.BlockSpec((seq, wb), lambda b, g: (b, g)),
        out_shape=jax.ShapeDtypeStruct((m, HEADS * HEAD_DV), BF16),
        scratch_shapes=[
            pltpu.VMEM((seq, LANES), F32),
            pltpu.VMEM((seq, LANES), F32),
            pltpu.VMEM((HB, nch, CHUNK), F32),
            pltpu.VMEM((HB, LANES, 2 * LANES), F32),
        ],
        compiler_params=pltpu.CompilerParams(dimension_semantics=("parallel", "arbitrary"),
                                             vmem_limit_bytes=VMEM_LIMIT),
        name="mlstm_core",
    )(proj, proj, proj, proj, ga, gat4, pcol, gate_b, norm_g.reshape(1, HEADS * HEAD_DV))


def _rope(x, cos, sin_lo, sin_hi):
    half = ROT_DIMS // 2
    return (x * cos + pltpu.roll(x, LANES - half, axis=1) * sin_lo
            + pltpu.roll(x, half, axis=1) * sin_hi)


def _diff_kernel(q_ref, k_ref, v_ref, tq_ref, tk_ref, lam_ref, ng_ref, o_ref,
                 kr_ref, vt_ref, q2_ref, m_ref, l_ref, acc_ref, *, seq, tq, dh, lambda_init):
    qi = pl.program_id(2)
    tk = tq

    @pl.when(qi == 0)
    def _():
        def prep_tile(t, carry):
            rows = pl.ds(pl.multiple_of(t * tk, tk), tk)
            for d in range(dh):
                cs = slice(d * LANES, (d + 1) * LANES)
                kr_ref[d, rows, :] = _rope(k_ref[rows, cs], tk_ref[0, rows, :], tk_ref[1, rows, :],
                                           tk_ref[2, rows, :]).astype(BF16)
                vt_ref[d, :, rows] = v_ref[rows, cs].T.astype(BF16)
            return carry
        lax.fori_loop(0, seq // tk, prep_tile, 0)

    lane = lax.broadcasted_iota(jnp.int32, (tq, LANES), 1)
    for d in range(dh):
        cs = slice(d * LANES, (d + 1) * LANES)
        q = _rope(q_ref[:, cs], tq_ref[0], tq_ref[1], tq_ref[2]) * DIFF_D ** -0.5
        q2_ref[d] = jnp.concatenate(
            [jnp.where(lane < DIFF_D, q, 0.0), jnp.where(lane >= DIFF_D, q, 0.0)],
            axis=0).astype(BF16)
    m_ref[...] = jnp.full(m_ref.shape, -jnp.inf, F32)
    l_ref[...] = jnp.zeros(l_ref.shape, F32)
    acc_ref[...] = jnp.zeros(acc_ref.shape, F32)

    def kv_block(j, masked):
        rows = pl.ds(pl.multiple_of(j * tk, tk), tk)
        st = [_dot_nt(kr_ref[d, rows, :], q2_ref[d]) for d in range(dh)]
        if masked:
            kpos = lax.broadcasted_iota(jnp.int32, (tk, 2 * tq), 0)
            qpos = lax.broadcasted_iota(jnp.int32, (tk, 2 * tq), 1)
            qpos = jnp.where(qpos >= tq, qpos - tq, qpos)
            shift = CHUNK.bit_length() - 1
            keep = jnp.right_shift(kpos, shift) <= jnp.right_shift(qpos, shift)
            st = [jnp.where(keep, s, -jnp.inf) for s in st]
        m_prev = [m_ref[d] for d in range(dh)]
        m_new = [jnp.maximum(m, jnp.max(s, axis=0, keepdims=True))
                 for m, s in zip(m_prev, st)]
        p = [jnp.exp(s - m) for s, m in zip(st, m_new)]
        pv = [_dot(vt_ref[d, :, rows], p[d].astype(BF16)) for d in range(dh)]
        for d in range(dh):
            alpha = jnp.exp(m_prev[d] - m_new[d])
            l_ref[d] = alpha * l_ref[d] + jnp.sum(p[d], axis=0, keepdims=True)
            acc_ref[d] = alpha * acc_ref[d] + pv[d]
            m_ref[d] = m_new[d]

    def body(j, carry):
        kv_block(j, False)
        return carry

    lax.fori_loop(0, qi, body, 0)
    kv_block(qi, True)

    lp = lam_ref[...]
    lam = (jnp.exp(jnp.sum(lp[0:1] * lp[1:2], axis=-1, keepdims=True))
           - jnp.exp(jnp.sum(lp[2:3] * lp[3:4], axis=-1, keepdims=True)) + lambda_init)
    for d in range(dh):
        out_t = acc_ref[d] * (1.0 / l_ref[d])
        o = (out_t[:, :tq] - lam * out_t[:, tq:]).T
        o_ref[:, d * LANES:(d + 1) * LANES] = (
            _rms(o, ng_ref[...]) * (1.0 - lambda_init)).astype(o_ref.dtype)


def _rope_tables(seq):
    half = ROT_DIMS // 2
    inv_freq = ROPE_THETA ** (-jnp.arange(0, ROT_DIMS, 2, dtype=F32) / ROT_DIMS)
    ang = jnp.arange(seq, dtype=F32)[:, None] * inv_freq[None, :]
    cos, sin = jnp.cos(ang), jnp.sin(ang)
    one = jnp.ones((seq, DIFF_D - ROT_DIMS), F32)
    zero = jnp.zeros((seq, DIFF_D - ROT_DIMS), F32)
    zh = jnp.zeros((seq, half), F32)
    cos_c = jnp.concatenate([cos, cos, one], axis=1)
    lo_c = jnp.concatenate([-sin, zh, zero], axis=1)
    hi_c = jnp.concatenate([zh, sin, zero], axis=1)
    rep = lambda t: jnp.concatenate([t, t], axis=1)
    return jnp.stack([rep(cos_c), rep(lo_c), rep(hi_c)], axis=0)


def _diff_core(proj, tables, lam_p, norm_g, lambda_init, *, batch, seq, tq=256, dh=4):
    m = batch * seq
    nq = seq // tq
    hg = HEADS // dh
    wb = dh * LANES
    kernel = functools.partial(_diff_kernel, seq=seq, tq=tq, dh=dh, lambda_init=lambda_init)
    return pl.pallas_call(
        kernel, grid=(batch, hg, nq),
        in_specs=[
            pl.BlockSpec((tq, wb), lambda b, h, i: (b * nq + i, h)),
            pl.BlockSpec((seq, wb), lambda b, h, i: (b, hg + h)),
            pl.BlockSpec((seq, wb), lambda b, h, i: (b, 2 * hg + h)),
            pl.BlockSpec((3, tq, LANES), lambda b, h, i: (0, i, 0)),
            pl.BlockSpec((3, seq, LANES), lambda b, h, i: (0, 0, 0)),
            pl.BlockSpec((4, DIFF_D), lambda b, h, i: (0, 0)),
            pl.BlockSpec((1, LANES), lambda b, h, i: (0, 0)),
        ],
        out_specs=pl.BlockSpec((tq, wb), lambda b, h, i: (b * nq + i, h)),
        out_shape=jax.ShapeDtypeStruct((m, HEADS * 2 * DIFF_D), BF16),
        scratch_shapes=[
            pltpu.VMEM((dh, seq, LANES), BF16),
            pltpu.VMEM((dh, LANES, seq), BF16),
            pltpu.VMEM((dh, 2 * tq, LANES), BF16),
            pltpu.VMEM((dh, 1, 2 * tq), F32),
            pltpu.VMEM((dh, 1, 2 * tq), F32),
            pltpu.VMEM((dh, LANES, 2 * tq), F32),
        ],
        compiler_params=pltpu.CompilerParams(
            dimension_semantics=("parallel", "parallel", "arbitrary"),
            vmem_limit_bytes=VMEM_LIMIT),
        name="diff_core",
    )(proj, proj, proj, tables, tables, lam_p, norm_g.reshape(1, 2 * DIFF_D))


def _out_proj_kernel(o_ref, w_ref, g_ref, x_ref, y_ref):
    y_ref[...] = x_ref[...] + _rms(_dot(o_ref[...], w_ref[...]), g_ref[...])


def _out_proj(o16, w16, g, x2, *, tm=1024):
    m, d = x2.shape
    return pl.pallas_call(
        _out_proj_kernel, grid=(m // tm,),
        in_specs=[pl.BlockSpec((tm, d), lambda i: (i, 0)),
                  pl.BlockSpec((d, d), lambda i: (0, 0)),
                  pl.BlockSpec((1, d), lambda i: (0, 0)),
                  pl.BlockSpec((tm, d), lambda i: (i, 0))],
        out_specs=pl.BlockSpec((tm, d), lambda i: (i, 0)),
        out_shape=jax.ShapeDtypeStruct((m, d), F32),
        compiler_params=pltpu.CompilerParams(dimension_semantics=("parallel",),
                                             vmem_limit_bytes=VMEM_LIMIT),
        name="out_proj",
    )(o16, w16, g, x2)


def _mlp_kernel(x_ref, g2_ref, w1_ref, w2_ref, g3_ref, y_ref, h_ref, acc_ref):
    j = pl.program_id(1)

    @pl.when(j == 0)
    def _():
        h_ref[...] = _rms(x_ref[...], g2_ref[...]).astype(BF16)
        acc_ref[...] = jnp.zeros(acc_ref.shape, F32)

    a = jnp.maximum(_dot(h_ref[...], w1_ref[...]), 0.0)
    acc_ref[...] += _dot((a * a).astype(BF16), w2_ref[...])

    @pl.when(j == pl.num_programs(1) - 1)
    def _():
        y_ref[...] = x_ref[...] + _rms(acc_ref[...], g3_ref[...])


def _mlp(x2, g2, w1_16, w2_16, g3, *, tm=1024, tf=1024):
    m, d = x2.shape
    f = w1_16.shape[1]
    return pl.pallas_call(
        _mlp_kernel, grid=(m // tm, f // tf),
        in_specs=[pl.BlockSpec((tm, d), lambda i, j: (i, 0)),
                  pl.BlockSpec((1, d), lambda i, j: (0, 0)),
                  pl.BlockSpec((d, tf), lambda i, j: (0, j)),
                  pl.BlockSpec((tf, d), lambda i, j: (j, 0)),
                  pl.BlockSpec((1, d), lambda i, j: (0, 0))],
        out_specs=pl.BlockSpec((tm, d), lambda i, j: (i, 0)),
        out_shape=jax.ShapeDtypeStruct((m, d), F32),
        scratch_shapes=[pltpu.VMEM((tm, d), BF16), pltpu.VMEM((tm, d), F32)],
        compiler_params=pltpu.CompilerParams(dimension_semantics=("parallel", "arbitrary"),
                                             vmem_limit_bytes=VMEM_LIMIT),
        name="mlp",
    )(x2, g2, w1_16, w2_16, g3)


def kernel(x, norm_g, mlp_w1, mlp_w2, gdn_w_in, gdn_conv, gdn_a_log, gdn_dt_bias, gdn_norm_g,
           gdn_w_out, mlstm_w_in, mlstm_gate_b, mlstm_norm_g, mlstm_w_out, diff_w_in,
           diff_lambda, diff_norm_g, diff_w_out):
    batch, seq, d = x.shape
    depth = norm_g.shape[0]
    x2 = x.reshape(batch * seq, d)
    tables = _rope_tables(seq)
    row = lambda v: v.reshape(1, -1)
    for i in range(depth):
        kind, j = i % 3, i // 3
        g_pre = row(norm_g[i, 0])
        if kind == 0:
            w_in = gdn_w_in[j]
            n_main = 4 * HEADS * HEAD_DV
            wc, wr = _gate_weights(w_in[:, n_main:n_main + HEADS], w_in[:, n_main + HEADS:])
            proj, ga, gat = _in_proj(x2, g_pre, w_in[:, :n_main].astype(BF16), wc, wr)
            o16 = _gdn_core(proj, ga, gat, gdn_conv[j], gdn_a_log[j], gdn_dt_bias[j],
                            gdn_norm_g[j], batch=batch, seq=seq)
            w_out = gdn_w_out[j]
        elif kind == 1:
            w_in = mlstm_w_in[j]
            n_main = 3 * HEADS * HEAD_DV
            wc, wr = _gate_weights(w_in[:, n_main:n_main + HEADS], w_in[:, n_main + HEADS:])
            proj, ga, gat = _in_proj(x2, g_pre, w_in[:, :n_main].astype(BF16), wc, wr)
            o16 = _mlstm_core(proj, ga, gat, mlstm_gate_b[j], mlstm_norm_g[j],
                              batch=batch, seq=seq)
            w_out = mlstm_w_out[j]
        else:
            lambda_init = 0.8 - 0.6 * math.exp(-0.3 * i)
            proj = _in_proj(x2, g_pre, diff_w_in[j].astype(BF16))
            o16 = _diff_core(proj, tables, diff_lambda[j], diff_norm_g[j], lambda_init,
                             batch=batch, seq=seq)
            w_out = diff_w_out[j]
        x2 = _out_proj(o16, w_out.astype(BF16), row(norm_g[i, 1]), x2)
        x2 = _mlp(x2, row(norm_g[i, 2]), mlp_w1[i].astype(BF16), mlp_w2[i].astype(BF16),
                  row(norm_g[i, 3]))
    return x2.reshape(batch, seq, d)
```

```python
import functools
import math

import jax
import jax.numpy as jnp
from jax import lax
from jax.experimental import pallas as pl
from jax.experimental.pallas import tpu as pltpu

F32 = jnp.float32
BF16 = jnp.bfloat16

D_MODEL = 1024
D_FF = 4 * D_MODEL
HEADS = 8
HEAD_DV = 128
CHUNK = 64
NORM_EPS = 1e-6
GATE_CAP = 15.0
GDN_CONV = 4
ROPE_THETA = 500000.0
DIFF_D = 64
ROT_DIMS = DIFF_D // 4
LANES = 128
ONES_ROWS = 16
LOG2_E = math.log2(math.e)
HB = 2
GROUPS = HEADS // HB
VMEM_LIMIT = 56 * 1024 * 1024

_NT = (((1,), (1,)), ((), ()))
_TN = (((0,), (0,)), ((), ()))


def _dot(a, b):
    return jnp.dot(a, b, preferred_element_type=F32)


def _dot_nt(a, b):
    return lax.dot_general(a, b, _NT, preferred_element_type=F32)


def _dot_tn(a, b):
    return lax.dot_general(a, b, _TN, preferred_element_type=F32)


def _rms(x, g):
    ms = jnp.mean(x * x, axis=-1, keepdims=True)
    return x * lax.rsqrt(ms + NORM_EPS) * g


def _sigmoid(x):
    return 1.0 / (1.0 + jnp.exp(-x))


def _softplus(x):
    return jnp.maximum(x, 0.0) + jnp.log1p(jnp.exp(-jnp.abs(x)))


def _soft_cap(x):
    return GATE_CAP * jnp.tanh(x / GATE_CAP)


def _split3(x):
    x1 = x.astype(BF16)
    r1 = x - x1.astype(F32)
    x2 = r1.astype(BF16)
    r2 = r1 - x2.astype(F32)
    return x1, x2, r2.astype(BF16)


def _cumsum_rows(x, tril16):
    a, b, c = _split3(x)
    return _dot(tril16, a) + _dot(tril16, b) + _dot(tril16, c)


def _cumsum_lanes(x, triu16):
    a, b, c = _split3(x)
    return _dot(a, triu16) + _dot(b, triu16) + _dot(c, triu16)


def _split2(x):
    hi = x.astype(BF16)
    return hi, (x - hi.astype(F32)).astype(BF16)


def _dot3(a, b):
    ah, al = _split2(a)
    bh, bl = _split2(b)
    return _dot(ah, bh) + _dot(ah, bl) + _dot(al, bh)


def _dot2(a, b):
    ah = a.astype(BF16)
    bh, bl = _split2(b)
    return _dot(ah, bh) + _dot(ah, bl)


def _unit_lower_inverse_minus_identity(a):
    n = a[0].shape[0]
    r = lax.broadcasted_iota(jnp.int32, (n, n), 0)
    c = lax.broadcasted_iota(jnp.int32, (n, n), 1)
    pair = jnp.right_shift(r, 1) == jnp.right_shift(c, 1)
    a = [x.astype(BF16).astype(F32) for x in a]
    inv_m1 = [-jnp.where(pair, x, 0.0) for x in a]
    s = 2
    while s < n:
        sh = s.bit_length() - 1
        merge = ((jnp.right_shift(r, sh + 1) == jnp.right_shift(c, sh + 1))
                 & (jnp.right_shift(r, sh) != jnp.right_shift(c, sh)))
        off = [jnp.where(merge, x, 0.0) for x in a]
        t = [_dot2(o, m) for o, m in zip(off, inv_m1)]
        y = [o + x for o, x in zip(off, t)]
        t = [_dot3(m, x) for m, x in zip(inv_m1, y)]
        inv_m1 = [m - (x + z) for m, x, z in zip(inv_m1, y, t)]
        s *= 2
    return inv_m1


def _tri_masks(n):
    r = lax.broadcasted_iota(jnp.int32, (n, n), 0)
    c = lax.broadcasted_iota(jnp.int32, (n, n), 1)
    return r >= c, r > c, r <= c


def _ones_where(mask):
    return jnp.where(mask, 1.0, 0.0).astype(BF16)


def _in_proj_kernel(x_ref, g_ref, w_ref, o_ref, h_ref):
    @pl.when(pl.program_id(1) == 0)
    def _():
        h_ref[...] = _rms(x_ref[...], g_ref[...]).astype(BF16)

    o_ref[...] = _dot(h_ref[...], w_ref[...]).astype(o_ref.dtype)


def _in_proj_gates_kernel(x_ref, g_ref, w_ref, wc_ref, wr_ref, o_ref, oc_ref, or_ref, h_ref):
    @pl.when(pl.program_id(1) == 0)
    def _():
        h = _rms(x_ref[...], g_ref[...]).astype(BF16)
        h_ref[...] = h
        oc_ref[...] = _dot(h, wc_ref[...])
        or_ref[...] = _dot_nt(wr_ref[...], h)

    o_ref[...] = _dot(h_ref[...], w_ref[...]).astype(o_ref.dtype)


def _in_proj(x2, g, w16, wc16=None, wr16=None, *, tm=512):
    m, d = x2.shape
    n = w16.shape[1]
    tn = n
    grid = (m // tm, n // tn)
    x_spec = pl.BlockSpec((tm, d), lambda i, j: (i, 0))
    g_spec = pl.BlockSpec((1, d), lambda i, j: (0, 0))
    w_spec = pl.BlockSpec((d, tn), lambda i, j: (0, j))
    o_spec = pl.BlockSpec((tm, tn), lambda i, j: (i, j))
    params = pltpu.CompilerParams(dimension_semantics=("parallel", "arbitrary"),
                                  vmem_limit_bytes=VMEM_LIMIT)
    if wc16 is None:
        return pl.pallas_call(
            _in_proj_kernel, grid=grid,
            in_specs=[x_spec, g_spec, w_spec], out_specs=o_spec,
            out_shape=jax.ShapeDtypeStruct((m, n), F32),
            scratch_shapes=[pltpu.VMEM((tm, d), BF16)],
            compiler_params=params, name="in_proj",
        )(x2, g, w16)
    nc = wc16.shape[1]
    nr = wr16.shape[0]
    return pl.pallas_call(
        _in_proj_gates_kernel, grid=grid,
        in_specs=[x_spec, g_spec, w_spec,
                  pl.BlockSpec((d, nc), lambda i, j: (0, 0)),
                  pl.BlockSpec((nr, d), lambda i, j: (0, 0))],
        out_specs=[o_spec,
                   pl.BlockSpec((tm, nc), lambda i, j: (i, 0)),
                   pl.BlockSpec((nr, tm), lambda i, j: (0, i))],
        out_shape=[jax.ShapeDtypeStruct((m, n), F32),
                   jax.ShapeDtypeStruct((m, nc), F32),
                   jax.ShapeDtypeStruct((nr, m), F32)],
        scratch_shapes=[pltpu.VMEM((tm, d), BF16)],
        compiler_params=params, name="in_proj_gates",
    )(x2, g, w16, wc16, wr16)


def _gate_weights(w_a, w_b):
    d = w_a.shape[0]
    a = w_a.reshape(d, GROUPS, HB)
    b = w_b.reshape(d, GROUPS, HB)
    pad = jnp.zeros((d, GROUPS, LANES - 2 * HB), w_a.dtype)
    wc = jnp.concatenate([a, b, pad], axis=2).reshape(d, GROUPS * LANES)
    wr = jnp.concatenate([w_a, w_b], axis=1).T
    return wc.astype(BF16), wr.astype(BF16)


def _group_rows(p_a, p_b):
    pad = jnp.zeros((GROUPS, LANES - HB), F32)
    ra = jnp.concatenate([p_a.reshape(GROUPS, HB), pad], axis=1)
    rb = jnp.concatenate([p_b.reshape(GROUPS, HB), pad], axis=1)
    return jnp.stack([ra, rb], axis=1)


def _gdn_kernel(q_ref, k_ref, v_ref, z_ref, gc_ref, gr_ref, cwq_ref, cwk_ref, cwv_ref,
                pcol_ref, prow_ref, ng_ref, o_ref,
                xp_ref, qn_ref, kn_ref, vn_ref, bcol_ref, gcol_ref, grow_ref,
                fold_ref, sadd_ref, oadd_ref, *, seq):
    nch = seq // CHUNK
    fold_rows = HEAD_DV + CHUNK
    grp = pl.program_id(1)
    tril, strict, triu = _tri_masks(CHUNK)
    tril16 = _ones_where(tril)
    triu16 = _ones_where(triu)

    alog_row = pcol_ref[0, 0:1, :]
    dtb_row = pcol_ref[0, 1:2, :]
    neg_decay_rate = -jnp.exp(alog_row)

    def gate_cols(c, carry):
        rows = pl.ds(pl.multiple_of(c * CHUNK, CHUNK), CHUNK)
        ga = gc_ref[rows, :]
        g = neg_decay_rate * _softplus(ga + dtb_row)
        gcol_ref[rows, :] = _cumsum_rows(g, tril16)
        bcol_ref[rows, :] = _sigmoid(ga)
        return carry

    lax.fori_loop(0, nch, gate_cols, 0, unroll=4)

    for i in range(HB):
        hh = grp * HB + i
        a_r = gr_ref[hh, 0]
        alog = jnp.full((nch, CHUNK), prow_ref[0, hh], F32)
        g_r = -jnp.exp(alog) * _softplus(a_r + prow_ref[1, hh])
        grow_ref[i] = _cumsum_lanes(g_r, triu16)

    tile = 256
    xp_ref[0:8, :] = jnp.zeros((8, LANES), F32)
    jobs = []
    for i in range(HB):
        cs = slice(i * LANES, (i + 1) * LANES)
        jobs.append((q_ref, cwq_ref, qn_ref, cs, HEAD_DV ** -0.5))
        jobs.append((k_ref, cwk_ref, kn_ref, cs, 1.0))
        jobs.append((v_ref, cwv_ref, vn_ref, cs, None))
    for src, cw_ref, dst, cs, scale in jobs:
        xp_ref[8:seq + 8, :] = src[:, cs]
        cw = cw_ref[:, cs]

        def conv_tile(t, carry, dst=dst, cs=cs, cw=cw, scale=scale):
            r0 = pl.multiple_of(t * tile, tile)
            y = cw[3:4, :] * xp_ref[pl.ds(r0 + 8, tile), :]
            for j in range(GDN_CONV - 1):
                y = y + cw[j:j + 1, :] * xp_ref[pl.ds(r0 + 5 + j, tile), :]
            y = y * _sigmoid(y)
            if scale is not None:
                y = y * lax.rsqrt(jnp.sum(y * y, axis=-1, keepdims=True) + NORM_EPS)
                if scale != 1.0:
                    y = y * scale
            dst[pl.ds(r0, tile), cs] = y
            return carry

        lax.fori_loop(0, seq // tile, conv_tile, 0, unroll=2)

    chunks_per_iter = 8

    def intra(cc, carry):
        items = []
        for u in range(chunks_per_iter):
            c = cc * chunks_per_iter + u
            rows = pl.ds(pl.multiple_of(c * CHUNK, CHUNK), CHUNK)
            for i in range(HB):
                items.append((c, rows, i, slice(i * LANES, (i + 1) * LANES)))
        q = [qn_ref[rows, cs] for _, rows, _, cs in items]
        k = [kn_ref[rows, cs] for _, rows, _, cs in items]
        v = [vn_ref[rows, cs] for _, rows, _, cs in items]
        beta = [bcol_ref[rows, HB + i:HB + i + 1] for _, rows, i, _ in items]
        gc = [gcol_ref[rows, i:i + 1] for _, rows, i, _ in items]
        gr = [grow_ref[i, pl.ds(c, 1), :] for c, _, i, _ in items]
        decay = [jnp.where(tril, jnp.exp(jnp.where(tril, a - b, 0.0)), 0.0) for a, b in zip(gc, gr)]
        kb = [a * b for a, b in zip(k, beta)]
        k16 = [a.astype(BF16) for a in k]
        g_kk = [_dot_nt(a.astype(BF16), b) for a, b in zip(kb, k16)]
        g_qk = [_dot_nt(a.astype(BF16), b) for a, b in zip(q, k16)]
        a_mat = [jnp.where(strict, a * d, 0.0) for a, d in zip(g_kk, decay)]
        inv_m1 = _unit_lower_inverse_minus_identity(a_mat)
        egc = [jnp.exp(a) for a in gc]
        rhs = [jnp.concatenate([a * b, c_ * e], axis=1)
               for a, b, c_, e in zip(v, beta, kb, egc)]
        sol = [a + _dot3(m, a) for m, a in zip(inv_m1, rhs)]
        sol16 = [a.astype(BF16) for a in sol]
        kst16 = [(a * jnp.exp(g_[CHUNK - 1:CHUNK, :] - g_)).astype(BF16) for a, g_ in zip(k, gc)]
        att16 = [jnp.where(tril, a * d, 0.0).astype(BF16) for a, d in zip(g_qk, decay)]
        kst_sol = [_dot_tn(a, b) for a, b in zip(kst16, sol16)]
        att_sol = [_dot(a, b) for a, b in zip(att16, sol16)]
        for n, (c, rows, i, cs) in enumerate(items):
            t_rows = pl.ds(pl.multiple_of(c * fold_rows, CHUNK), HEAD_DV)
            q_rows = pl.ds(pl.multiple_of(c * fold_rows + HEAD_DV, CHUNK), CHUNK)
            fold_ref[t_rows, cs] = (-kst_sol[n][:, LANES:]).astype(BF16)
            fold_ref[q_rows, cs] = (q[n] * egc[n] - att_sol[n][:, LANES:]).astype(BF16)
            sadd_ref[pl.ds(pl.multiple_of(c * HEAD_DV, HEAD_DV), HEAD_DV), cs] = kst_sol[n][:, :LANES]
            oadd_ref[rows, cs] = att_sol[n][:, :LANES]
        return carry

    lax.fori_loop(0, nch // chunks_per_iter, intra, 0)

    ng = ng_ref[...]

    def step(c, states):
        r0 = pl.multiple_of(c * CHUNK, CHUNK)
        rows = pl.ds(r0, CHUNK)
        heads = range(HB)
        cols = [slice(i * LANES, (i + 1) * LANES) for i in heads]
        f_rows = pl.ds(pl.multiple_of(c * fold_rows, CHUNK), fold_rows)
        s_rows = pl.ds(pl.multiple_of(c * HEAD_DV, HEAD_DV), HEAD_DV)
        r = [_dot(fold_ref[f_rows, cols[i]], states[i].astype(BF16)) for i in heads]
        new_states = []
        for i in heads:
            egl = jnp.exp(gcol_ref[pl.ds(r0 + CHUNK - 1, 1), i:i + 1])
            new_states.append(states[i] * egl + r[i][:HEAD_DV] + sadd_ref[s_rows, cols[i]])
            z = z_ref[rows, cols[i]]
            o = r[i][HEAD_DV:] + oadd_ref[rows, cols[i]]
            o_ref[rows, cols[i]] = (_rms(o, ng) * (z * _sigmoid(z))).astype(o_ref.dtype)
        return tuple(new_states)

    init = tuple(jnp.zeros((HEAD_DV, HEAD_DV), F32) for _ in range(HB))
    lax.fori_loop(0, nch, step, init)


def _gdn_core(proj, ga, gat, conv_w, a_log, dt_bias, norm_g, *, batch, seq):
    m = batch * seq
    nch = seq // CHUNK
    wb = HB * LANES
    g_ = GROUPS
    gat4 = gat.reshape(2 * HEADS, batch, nch, CHUNK)
    pcol = _group_rows(a_log, dt_bias)
    prow = jnp.stack([a_log, dt_bias], axis=0)
    kernel = functools.partial(_gdn_kernel, seq=seq)
    return pl.pallas_call(
        kernel, grid=(batch, g_),
        in_specs=[
            pl.BlockSpec((seq, wb), lambda b, g: (b, g)),
            pl.BlockSpec((seq, wb), lambda b, g: (b, g_ + g)),
            pl.BlockSpec((seq, wb), lambda b, g: (b, 2 * g_ + g)),
            pl.BlockSpec((seq, wb), lambda b, g: (b, 3 * g_ + g)),
            pl.BlockSpec((seq, LANES), lambda b, g: (b, g)),
            pl.BlockSpec((2 * HEADS, 1, nch, CHUNK), lambda b, g: (0, b, 0, 0)),
            pl.BlockSpec((GDN_CONV, wb), lambda b, g: (0, g)),
            pl.BlockSpec((GDN_CONV, wb), lambda b, g: (0, g_ + g)),
            pl.BlockSpec((GDN_CONV, wb), lambda b, g: (0, 2 * g_ + g)),
            pl.BlockSpec((1, 2, LANES), lambda b, g: (g, 0, 0)),
            pl.BlockSpec(memory_space=pltpu.SMEM),
            pl.BlockSpec((1, LANES), lambda b, g: (0, 0)),
        ],
        out_specs=pl.BlockSpec((seq, wb), lambda b, g: (b, g)),
        out_shape=jax.ShapeDtypeStruct((m, HEADS * HEAD_DV), BF16),
        scratch_shapes=[
            pltpu.VMEM((seq + 8, LANES), F32),
            pltpu.VMEM((seq, wb), F32),
            pltpu.VMEM((seq, wb), F32),
            pltpu.VMEM((seq, wb), F32),
            pltpu.VMEM((seq, LANES), F32),
            pltpu.VMEM((seq, LANES), F32),
            pltpu.VMEM((HB, nch, CHUNK), F32),
            pltpu.VMEM((nch * (HEAD_DV + CHUNK), wb), BF16),
            pltpu.VMEM((nch * HEAD_DV, wb), F32),
            pltpu.VMEM((seq, wb), F32),
        ],
        compiler_params=pltpu.CompilerParams(dimension_semantics=("parallel", "arbitrary"),
                                             vmem_limit_bytes=VMEM_LIMIT),
        name="gdn_core",
    )(proj, proj, proj, proj, ga, gat4, conv_w, conv_w, conv_w, pcol, prow,
      norm_g.reshape(1, HEAD_DV))


def _mlstm_kernel(q_ref, k_ref, v_ref, og_ref, gc_ref, gr_ref, pcol_ref, prow_ref, ng_ref,
                  o_ref, icol_ref, bcol_ref, drow_ref, c_ref, *, seq):
    nch = seq // CHUNK
    dk = HEAD_DV // 2
    grp = pl.program_id(1)
    tril, _, triu = _tri_masks(CHUNK)
    tril16 = _ones_where(tril)
    triu16 = _ones_where(triu)
    bias_row = pcol_ref[0, 0:1, :]

    def gate_cols(c, carry):
        rows = pl.ds(pl.multiple_of(c * CHUNK, CHUNK), CHUNK)
        pre = _soft_cap(gc_ref[rows, :] + bias_row)
        icol_ref[rows, :] = pre
        bcol_ref[rows, :] = _cumsum_rows(-_softplus(-pre), tril16)
        return carry

    lax.fori_loop(0, nch, gate_cols, 0, unroll=4)

    for i in range(HB):
        hh = grp * HB + i
        i_r = _soft_cap(gr_ref[hh, 0] + prow_ref[0, hh])
        f_r = _soft_cap(gr_ref[HEADS + hh, 0] + prow_ref[1, hh])
        drow_ref[i] = i_r - _cumsum_lanes(-_softplus(-f_r), triu16)

    lane = lax.broadcasted_iota(jnp.int32, (CHUNK, LANES), 1)
    ones_col = _ones_where(lane == 0)
    c_ref[...] = jnp.zeros(c_ref.shape, F32)

    chunks_per_iter = 4
    head_lanes = [(lane >= i * dk) & (lane < (i + 1) * dk) for i in range(HB)]
    cols = [slice(i * LANES, (i + 1) * LANES) for i in range(HB)]

    def step(cc, m_states):
        m_st = list(m_states)
        items = [(u, i) for u in range(chunks_per_iter) for i in range(HB)]
        chunk = [cc * chunks_per_iter + u for u in range(chunks_per_iter)]
        rows = [pl.ds(pl.multiple_of(c * CHUNK, CHUNK), CHUNK) for c in chunk]
        qp = [q_ref[r, :] for r in rows]
        kp = [k_ref[r, :] for r in rows]
        kp16 = [x.astype(BF16) for x in kp]
        bcum = [bcol_ref[rows[u], HB + i:HB + i + 1] for u, i in items]
        ig = [icol_ref[rows[u], i:i + 1] for u, i in items]
        dlog = [jnp.where(tril, b + drow_ref[i, pl.ds(chunk[u], 1), :], -jnp.inf)
                for b, (u, i) in zip(bcum, items)]
        m_intra = [jnp.max(d, axis=-1, keepdims=True) for d in dlog]
        q16 = [(jnp.where(head_lanes[i], qp[u], 0.0) * dk ** -0.5).astype(BF16) for u, i in items]
        s_raw = [_dot_nt(q, kp16[u]) for q, (u, i) in zip(q16, items)]
        g = [b[CHUNK - 1:CHUNK, :] for b in bcum]
        a = [g_ - b + i_ for g_, b, i_ in zip(g, bcum, ig)]
        a_max = [jnp.max(x, axis=0, keepdims=True) for x in a]
        m_out, w_inter, w_old, m_new = [], [], [], []
        for n, (u, i) in enumerate(items):
            m_inter = bcum[n] + m_st[i]
            m_out.append(jnp.maximum(m_inter, m_intra[n]))
            w_inter.append(jnp.exp(m_inter - m_out[n]))
            m_new.append(jnp.maximum(g[n] + m_st[i], a_max[n]))
            w_old.append(jnp.exp(g[n] + m_st[i] - m_new[n]))
            m_st[i] = m_new[n]
        vaug = [jnp.concatenate([v_ref[rows[u], cols[i]].astype(BF16), ones_col], axis=1)
                for u, i in items]
        s16 = [(s * jnp.exp(d - m)).astype(BF16) for s, d, m in zip(s_raw, dlog, m_out)]
        sv = [_dot(s, v) for s, v in zip(s16, vaug)]
        kw = [(jnp.where(head_lanes[i], kp[u], 0.0) * jnp.exp(a[n] - m_new[n])).astype(BF16)
              for n, (u, i) in enumerate(items)]
        kv = [_dot_tn(k_, v) for k_, v in zip(kw, vaug)]
        cst = [c_ref[i] for i in range(HB)]
        for n, (u, i) in enumerate(items):
            nd = w_inter[n] * _dot(q16[n], cst[i].astype(BF16)) + sv[n]
            den = nd[:, LANES:LANES + 1]
            h = nd[:, :LANES] * (1.0 / jnp.maximum(jnp.abs(den), jnp.exp(-m_out[n])))
            cst[i] = w_old[n] * cst[i] + kv[n]
            hn = _rms(h, ng_ref[:, cols[i]])
            o_ref[rows[u], cols[i]] = (_sigmoid(og_ref[rows[u], cols[i]]) * hn).astype(o_ref.dtype)
        for i in range(HB):
            c_ref[i] = cst[i]
        return tuple(m_st)

    lax.fori_loop(0, nch // chunks_per_iter, step,
                  tuple(jnp.zeros((1, 1), F32) for _ in range(HB)))


def _mlstm_core(proj, ga, gat, gate_b, norm_g, *, batch, seq):
    m = batch * seq
    nch = seq // CHUNK
    wb = HB * LANES
    g_ = GROUPS
    gat4 = gat.reshape(2 * HEADS, batch, nch, CHUNK)
    pad = jnp.zeros((GROUPS, LANES - 2 * HB), F32)
    pcol = jnp.concatenate([gate_b[0].reshape(GROUPS, HB), gate_b[1].reshape(GROUPS, HB), pad],
                           axis=1).reshape(GROUPS, 1, LANES)
    kernel = functools.partial(_mlstm_kernel, seq=seq)
    return pl.pallas_call(
        kernel, grid=(batch, g_),
        in_specs=[
            pl.BlockSpec((seq, LANES), lambda b, g: (b, g)),
            pl.BlockSpec((seq, LANES), lambda b, g: (b, g_ + g)),
            pl.BlockSpec((seq, wb), lambda b, g: (b, g_ + g)),
            pl.BlockSpec((seq, wb), lambda b, g: (b, 2 * g_ + g)),
            pl.BlockSpec((seq, LANES), lambda b, g: (b, g)),
            pl.BlockSpec((2 * HEADS, 1, nch, CHUNK), lambda b, g: (0, b, 0, 0)),
            pl.BlockSpec((1, 1, LANES), lambda b, g: (g, 0, 0)),
            pl.BlockSpec(memory_space=pltpu.SMEM),
            pl.BlockSpec((1, wb), lambda b, g: (0, g)),
        ],
        out_specs=pl.BlockSpec((seq, wb), lambda b, g: (b, g)),
        out_shape=jax.ShapeDtypeStruct((m, HEADS * HEAD_DV), BF16),
        scratch_shapes=[
            pltpu.VMEM((seq, LANES), F32),
            pltpu.VMEM((seq, LANES), F32),
            pltpu.VMEM((HB, nch, CHUNK), F32),
            pltpu.VMEM((HB, LANES, 2 * LANES), F32),
        ],
        compiler_params=pltpu.CompilerParams(dimension_semantics=("parallel", "arbitrary"),
                                             vmem_limit_bytes=VMEM_LIMIT),
        name="mlstm_core",
    )(proj, proj, proj, proj, ga, gat4, pcol, gate_b, norm_g.reshape(1, HEADS * HEAD_DV))


def _rope(x, cos, sin_lo, sin_hi):
    half = ROT_DIMS // 2
    return (x * cos + pltpu.roll(x, LANES - half, axis=1) * sin_lo
            + pltpu.roll(x, half, axis=1) * sin_hi)


def _diff_kernel(q_ref, k_ref, v_ref, tq_ref, tk_ref, lam_ref, ng_ref, o_ref,
                 kr_ref, vt_ref, q2_ref, m_ref, acc_ref, *, seq, tq, dh, lambda_init):
    qi = pl.program_id(2)
    tk = tq

    @pl.when(qi == 0)
    def _():
        def prep_tile(t, carry):
            rows = pl.ds(pl.multiple_of(t * tk, tk), tk)
            for d in range(dh):
                cs = slice(d * LANES, (d + 1) * LANES)
                kr_ref[d, rows, :] = _rope(k_ref[rows, cs], tk_ref[0, rows, :], tk_ref[1, rows, :],
                                           tk_ref[2, rows, :]).astype(BF16)
                vt_ref[d, 0:LANES, rows] = v_ref[rows, cs].T.astype(BF16)
                vt_ref[d, LANES:LANES + ONES_ROWS, rows] = _ones_where(
                    lax.broadcasted_iota(jnp.int32, (ONES_ROWS, tk), 0) == 0)
            return carry
        lax.fori_loop(0, seq // tk, prep_tile, 0)

    lane = lax.broadcasted_iota(jnp.int32, (tq, LANES), 1)
    for d in range(dh):
        cs = slice(d * LANES, (d + 1) * LANES)
        q = _rope(q_ref[:, cs], tq_ref[0], tq_ref[1], tq_ref[2]) * (DIFF_D ** -0.5 * LOG2_E)
        q2_ref[d] = jnp.concatenate(
            [jnp.where(lane < DIFF_D, q, 0.0), jnp.where(lane >= DIFF_D, q, 0.0)],
            axis=0).astype(BF16)
    m_ref[...] = jnp.full(m_ref.shape, -jnp.inf, F32)
    acc_ref[...] = jnp.zeros(acc_ref.shape, F32)

    def kv_block(j, masked):
        rows = pl.ds(pl.multiple_of(j * tk, tk), tk)
        st = [_dot_nt(kr_ref[d, rows, :], q2_ref[d]) for d in range(dh)]
        if masked:
            kpos = lax.broadcasted_iota(jnp.int32, (tk, 2 * tq), 0)
            qpos = lax.broadcasted_iota(jnp.int32, (tk, 2 * tq), 1)
            qpos = jnp.where(qpos >= tq, qpos - tq, qpos)
            shift = CHUNK.bit_length() - 1
            keep = jnp.right_shift(kpos, shift) <= jnp.right_shift(qpos, shift)
            st = [jnp.where(keep, s, -jnp.inf) for s in st]
        m_prev = [m_ref[d] for d in range(dh)]
        m_new = [jnp.maximum(m, jnp.max(s, axis=0, keepdims=True))
                 for m, s in zip(m_prev, st)]
        p16 = [jnp.exp2(s - m).astype(BF16) for s, m in zip(st, m_new)]
        pv = [_dot(vt_ref[d, :, rows], p16[d]) for d in range(dh)]
        for d in range(dh):
            acc_ref[d] = jnp.exp2(m_prev[d] - m_new[d]) * acc_ref[d] + pv[d]
            m_ref[d] = m_new[d]

    def body(j, carry):
        kv_block(j, False)
        return carry

    lax.fori_loop(0, qi, body, 0)
    kv_block(qi, True)

    lp = lam_ref[...]
    lam = (jnp.exp(jnp.sum(lp[0:1] * lp[1:2], axis=-1, keepdims=True))
           - jnp.exp(jnp.sum(lp[2:3] * lp[3:4], axis=-1, keepdims=True)) + lambda_init)
    for d in range(dh):
        acc = acc_ref[d]
        out_t = acc[:LANES] * (1.0 / acc[LANES:LANES + 1])
        o = (out_t[:, :tq] - lam * out_t[:, tq:]).T
        o_ref[:, d * LANES:(d + 1) * LANES] = (
            _rms(o, ng_ref[...]) * (1.0 - lambda_init)).astype(o_ref.dtype)


def _rope_tables(seq):
    half = ROT_DIMS // 2
    inv_freq = ROPE_THETA ** (-jnp.arange(0, ROT_DIMS, 2, dtype=F32) / ROT_DIMS)
    ang = jnp.arange(seq, dtype=F32)[:, None] * inv_freq[None, :]
    cos, sin = jnp.cos(ang), jnp.sin(ang)
    one = jnp.ones((seq, DIFF_D - ROT_DIMS), F32)
    zero = jnp.zeros((seq, DIFF_D - ROT_DIMS), F32)
    zh = jnp.zeros((seq, half), F32)
    cos_c = jnp.concatenate([cos, cos, one], axis=1)
    lo_c = jnp.concatenate([-sin, zh, zero], axis=1)
    hi_c = jnp.concatenate([zh, sin, zero], axis=1)
    rep = lambda t: jnp.concatenate([t, t], axis=1)
    return jnp.stack([rep(cos_c), rep(lo_c), rep(hi_c)], axis=0)


def _diff_core(proj, tables, lam_p, norm_g, lambda_init, *, batch, seq, tq=256, dh=4):
    m = batch * seq
    nq = seq // tq
    hg = HEADS // dh
    wb = dh * LANES
    kernel = functools.partial(_diff_kernel, seq=seq, tq=tq, dh=dh, lambda_init=lambda_init)
    return pl.pallas_call(
        kernel, grid=(batch, hg, nq),
        in_specs=[
            pl.BlockSpec((tq, wb), lambda b, h, i: (b * nq + i, h)),
            pl.BlockSpec((seq, wb), lambda b, h, i: (b, hg + h)),
            pl.BlockSpec((seq, wb), lambda b, h, i: (b, 2 * hg + h)),
            pl.BlockSpec((3, tq, LANES), lambda b, h, i: (0, i, 0)),
            pl.BlockSpec((3, seq, LANES), lambda b, h, i: (0, 0, 0)),
            pl.BlockSpec((4, DIFF_D), lambda b, h, i: (0, 0)),
            pl.BlockSpec((1, LANES), lambda b, h, i: (0, 0)),
        ],
        out_specs=pl.BlockSpec((tq, wb), lambda b, h, i: (b * nq + i, h)),
        out_shape=jax.ShapeDtypeStruct((m, HEADS * 2 * DIFF_D), BF16),
        scratch_shapes=[
            pltpu.VMEM((dh, seq, LANES), BF16),
            pltpu.VMEM((dh, LANES + ONES_ROWS, seq), BF16),
            pltpu.VMEM((dh, 2 * tq, LANES), BF16),
            pltpu.VMEM((dh, 1, 2 * tq), F32),
            pltpu.VMEM((dh, LANES + ONES_ROWS, 2 * tq), F32),
        ],
        compiler_params=pltpu.CompilerParams(
            dimension_semantics=("parallel", "parallel", "arbitrary"),
            vmem_limit_bytes=VMEM_LIMIT),
        name="diff_core",
    )(proj, proj, proj, tables, tables, lam_p, norm_g.reshape(1, 2 * DIFF_D))


def _out_proj_kernel(o_ref, w_ref, g_ref, x_ref, y_ref):
    y_ref[...] = x_ref[...] + _rms(_dot(o_ref[...], w_ref[...]), g_ref[...])


def _out_proj(o16, w16, g, x2, *, tm=1024):
    m, d = x2.shape
    return pl.pallas_call(
        _out_proj_kernel, grid=(m // tm,),
        in_specs=[pl.BlockSpec((tm, d), lambda i: (i, 0)),
                  pl.BlockSpec((d, d), lambda i: (0, 0)),
                  pl.BlockSpec((1, d), lambda i: (0, 0)),
                  pl.BlockSpec((tm, d), lambda i: (i, 0))],
        out_specs=pl.BlockSpec((tm, d), lambda i: (i, 0)),
        out_shape=jax.ShapeDtypeStruct((m, d), F32),
        compiler_params=pltpu.CompilerParams(dimension_semantics=("parallel",),
                                             vmem_limit_bytes=VMEM_LIMIT),
        name="out_proj",
    )(o16, w16, g, x2)


def _mlp_kernel(x_ref, g2_ref, w1_ref, w2_ref, g3_ref, y_ref, h_ref, acc_ref):
    j = pl.program_id(1)

    @pl.when(j == 0)
    def _():
        h_ref[...] = _rms(x_ref[...], g2_ref[...]).astype(BF16)
        acc_ref[...] = jnp.zeros(acc_ref.shape, F32)

    a = jnp.maximum(_dot(h_ref[...], w1_ref[...]), 0.0)
    acc_ref[...] += _dot((a * a).astype(BF16), w2_ref[...])

    @pl.when(j == pl.num_programs(1) - 1)
    def _():
        y_ref[...] = x_ref[...] + _rms(acc_ref[...], g3_ref[...])


def _mlp(x2, g2, w1_16, w2_16, g3, *, tm=1024, tf=1024):
    m, d = x2.shape
    f = w1_16.shape[1]
    return pl.pallas_call(
        _mlp_kernel, grid=(m // tm, f // tf),
        in_specs=[pl.BlockSpec((tm, d), lambda i, j: (i, 0)),
                  pl.BlockSpec((1, d), lambda i, j: (0, 0)),
                  pl.BlockSpec((d, tf), lambda i, j: (0, j)),
                  pl.BlockSpec((tf, d), lambda i, j: (j, 0)),
                  pl.BlockSpec((1, d), lambda i, j: (0, 0))],
        out_specs=pl.BlockSpec((tm, d), lambda i, j: (i, 0)),
        out_shape=jax.ShapeDtypeStruct((m, d), F32),
        scratch_shapes=[pltpu.VMEM((tm, d), BF16), pltpu.VMEM((tm, d), F32)],
        compiler_params=pltpu.CompilerParams(dimension_semantics=("parallel", "arbitrary"),
                                             vmem_limit_bytes=VMEM_LIMIT),
        name="mlp",
    )(x2, g2, w1_16, w2_16, g3)


def kernel(x, norm_g, mlp_w1, mlp_w2, gdn_w_in, gdn_conv, gdn_a_log, gdn_dt_bias, gdn_norm_g,
           gdn_w_out, mlstm_w_in, mlstm_gate_b, mlstm_norm_g, mlstm_w_out, diff_w_in,
           diff_lambda, diff_norm_g, diff_w_out):
    batch, seq, d = x.shape
    depth = norm_g.shape[0]
    x2 = x.reshape(batch * seq, d)
    tables = _rope_tables(seq)
    row = lambda v: v.reshape(1, -1)
    for i in range(depth):
        kind, j = i % 3, i // 3
        g_pre = row(norm_g[i, 0])
        if kind == 0:
            w_in = gdn_w_in[j]
            n_main = 4 * HEADS * HEAD_DV
            wc, wr = _gate_weights(w_in[:, n_main:n_main + HEADS], w_in[:, n_main + HEADS:])
            proj, ga, gat = _in_proj(x2, g_pre, w_in[:, :n_main].astype(BF16), wc, wr)
            o16 = _gdn_core(proj, ga, gat, gdn_conv[j], gdn_a_log[j], gdn_dt_bias[j],
                            gdn_norm_g[j], batch=batch, seq=seq)
            w_out = gdn_w_out[j]
        elif kind == 1:
            w_in = mlstm_w_in[j]
            n_main = 3 * HEADS * HEAD_DV
            wc, wr = _gate_weights(w_in[:, n_main:n_main + HEADS], w_in[:, n_main + HEADS:])
            proj, ga, gat = _in_proj(x2, g_pre, w_in[:, :n_main].astype(BF16), wc, wr)
            o16 = _mlstm_core(proj, ga, gat, mlstm_gate_b[j], mlstm_norm_g[j],
                              batch=batch, seq=seq)
            w_out = mlstm_w_out[j]
        else:
            lambda_init = 0.8 - 0.6 * math.exp(-0.3 * i)
            proj = _in_proj(x2, g_pre, diff_w_in[j].astype(BF16))
            o16 = _diff_core(proj, tables, diff_lambda[j], diff_norm_g[j], lambda_init,
                             batch=batch, seq=seq)
            w_out = diff_w_out[j]
        x2 = _out_proj(o16, w_out.astype(BF16), row(norm_g[i, 1]), x2)
        x2 = _mlp(x2, row(norm_g[i, 2]), mlp_w1[i].astype(BF16), mlp_w2[i].astype(BF16),
                  row(norm_g[i, 3]))
    return x2.reshape(batch, seq, d)
```

```python
import functools
import math

import jax
import jax.numpy as jnp
from jax import lax
from jax.experimental import pallas as pl
from jax.experimental.pallas import tpu as pltpu

F32 = jnp.float32
BF16 = jnp.bfloat16

D_MODEL = 1024
D_FF = 4 * D_MODEL
HEADS = 8
HEAD_DV = 128
CHUNK = 64
NORM_EPS = 1e-6
GATE_CAP = 15.0
GDN_CONV = 4
ROPE_THETA = 500000.0
DIFF_D = 64
ROT_DIMS = DIFF_D // 4
LANES = 128
ONES_ROWS = 16
LOG2_E = math.log2(math.e)
HB = 2
GROUPS = HEADS // HB
VMEM_LIMIT = 56 * 1024 * 1024

_NT = (((1,), (1,)), ((), ()))
_TN = (((0,), (0,)), ((), ()))


def _dot(a, b):
    return jnp.dot(a, b, preferred_element_type=F32)


def _dot_nt(a, b):
    return lax.dot_general(a, b, _NT, preferred_element_type=F32)


def _dot_tn(a, b):
    return lax.dot_general(a, b, _TN, preferred_element_type=F32)


def _rms(x, g):
    ms = jnp.mean(x * x, axis=-1, keepdims=True)
    return x * lax.rsqrt(ms + NORM_EPS) * g


def _sigmoid(x):
    return 1.0 / (1.0 + jnp.exp(-x))


def _softplus(x):
    return jnp.maximum(x, 0.0) + jnp.log1p(jnp.exp(-jnp.abs(x)))


def _soft_cap(x):
    return GATE_CAP * jnp.tanh(x / GATE_CAP)


def _split3(x):
    x1 = x.astype(BF16)
    r1 = x - x1.astype(F32)
    x2 = r1.astype(BF16)
    r2 = r1 - x2.astype(F32)
    return x1, x2, r2.astype(BF16)


def _cumsum_rows(x, tril16):
    a, b, c = _split3(x)
    return _dot(tril16, a) + _dot(tril16, b) + _dot(tril16, c)


def _cumsum_lanes(x, triu16):
    a, b, c = _split3(x)
    return _dot(a, triu16) + _dot(b, triu16) + _dot(c, triu16)


def _split2(x):
    hi = x.astype(BF16)
    return hi, (x - hi.astype(F32)).astype(BF16)


def _dot3(a, b):
    ah, al = _split2(a)
    bh, bl = _split2(b)
    return _dot(ah, bh) + _dot(ah, bl) + _dot(al, bh)


def _dot2(a, b):
    ah = a.astype(BF16)
    bh, bl = _split2(b)
    return _dot(ah, bh) + _dot(ah, bl)


def _unit_lower_inverse_minus_identity(a, between_levels=None):
    n = a[0].shape[0]
    r = lax.broadcasted_iota(jnp.int32, (n, n), 0)
    c = lax.broadcasted_iota(jnp.int32, (n, n), 1)
    pair = jnp.right_shift(r, 1) == jnp.right_shift(c, 1)
    a = [x.astype(BF16).astype(F32) for x in a]
    inv_m1 = [-jnp.where(pair, x, 0.0) for x in a]
    s = 2
    while s < n:
        sh = s.bit_length() - 1
        merge = ((jnp.right_shift(r, sh + 1) == jnp.right_shift(c, sh + 1))
                 & (jnp.right_shift(r, sh) != jnp.right_shift(c, sh)))
        off = [jnp.where(merge, x, 0.0) for x in a]
        t = [_dot2(o, m) for o, m in zip(off, inv_m1)]
        y = [o + x for o, x in zip(off, t)]
        t = [_dot3(m, x) for m, x in zip(inv_m1, y)]
        inv_m1 = [m - (x + z) for m, x, z in zip(inv_m1, y, t)]
        if between_levels is not None:
            between_levels()
        s *= 2
    return inv_m1


def _tri_masks(n):
    r = lax.broadcasted_iota(jnp.int32, (n, n), 0)
    c = lax.broadcasted_iota(jnp.int32, (n, n), 1)
    return r >= c, r > c, r <= c


def _ones_where(mask):
    return jnp.where(mask, 1.0, 0.0).astype(BF16)


def _in_proj_kernel(x_ref, g_ref, w_ref, o_ref, h_ref):
    @pl.when(pl.program_id(1) == 0)
    def _():
        h_ref[...] = _rms(x_ref[...], g_ref[...]).astype(BF16)

    o_ref[...] = _dot(h_ref[...], w_ref[...]).astype(o_ref.dtype)


def _in_proj_gates_kernel(x_ref, g_ref, w_ref, wc_ref, wr_ref, o_ref, oc_ref, or_ref, h_ref):
    @pl.when(pl.program_id(1) == 0)
    def _():
        h = _rms(x_ref[...], g_ref[...]).astype(BF16)
        h_ref[...] = h
        oc_ref[...] = _dot(h, wc_ref[...])
        or_ref[...] = _dot_nt(wr_ref[...], h)

    o_ref[...] = _dot(h_ref[...], w_ref[...]).astype(o_ref.dtype)


def _in_proj(x2, g, w16, wc16=None, wr16=None, *, tm=512):
    m, d = x2.shape
    n = w16.shape[1]
    tn = n
    grid = (m // tm, n // tn)
    x_spec = pl.BlockSpec((tm, d), lambda i, j: (i, 0))
    g_spec = pl.BlockSpec((1, d), lambda i, j: (0, 0))
    w_spec = pl.BlockSpec((d, tn), lambda i, j: (0, j))
    o_spec = pl.BlockSpec((tm, tn), lambda i, j: (i, j))
    params = pltpu.CompilerParams(dimension_semantics=("parallel", "arbitrary"),
                                  vmem_limit_bytes=VMEM_LIMIT)
    if wc16 is None:
        return pl.pallas_call(
            _in_proj_kernel, grid=grid,
            in_specs=[x_spec, g_spec, w_spec], out_specs=o_spec,
            out_shape=jax.ShapeDtypeStruct((m, n), F32),
            scratch_shapes=[pltpu.VMEM((tm, d), BF16)],
            compiler_params=params, name="in_proj",
        )(x2, g, w16)
    nc = wc16.shape[1]
    nr = wr16.shape[0]
    return pl.pallas_call(
        _in_proj_gates_kernel, grid=grid,
        in_specs=[x_spec, g_spec, w_spec,
                  pl.BlockSpec((d, nc), lambda i, j: (0, 0)),
                  pl.BlockSpec((nr, d), lambda i, j: (0, 0))],
        out_specs=[o_spec,
                   pl.BlockSpec((tm, nc), lambda i, j: (i, 0)),
                   pl.BlockSpec((nr, tm), lambda i, j: (0, i))],
        out_shape=[jax.ShapeDtypeStruct((m, n), F32),
                   jax.ShapeDtypeStruct((m, nc), F32),
                   jax.ShapeDtypeStruct((nr, m), F32)],
        scratch_shapes=[pltpu.VMEM((tm, d), BF16)],
        compiler_params=params, name="in_proj_gates",
    )(x2, g, w16, wc16, wr16)


def _gate_weights(w_a, w_b):
    d = w_a.shape[0]
    a = w_a.reshape(d, GROUPS, HB)
    b = w_b.reshape(d, GROUPS, HB)
    pad = jnp.zeros((d, GROUPS, LANES - 2 * HB), w_a.dtype)
    wc = jnp.concatenate([a, b, pad], axis=2).reshape(d, GROUPS * LANES)
    wr = jnp.concatenate([w_a, w_b], axis=1).T
    return wc.astype(BF16), wr.astype(BF16)


def _group_rows(p_a, p_b):
    pad = jnp.zeros((GROUPS, LANES - HB), F32)
    ra = jnp.concatenate([p_a.reshape(GROUPS, HB), pad], axis=1)
    rb = jnp.concatenate([p_b.reshape(GROUPS, HB), pad], axis=1)
    return jnp.stack([ra, rb], axis=1)


def _gdn_kernel(q_ref, k_ref, v_ref, z_ref, gc_ref, gr_ref, cwq_ref, cwk_ref, cwv_ref,
                pcol_ref, prow_ref, ng_ref, o_ref,
                xp_ref, qn_ref, kn_ref, vn_ref, bcol_ref, gcol_ref, grow_ref,
                fold_ref, sadd_ref, oadd_ref, *, seq):
    nch = seq // CHUNK
    fold_rows = HEAD_DV + CHUNK
    grp = pl.program_id(1)
    tril, strict, triu = _tri_masks(CHUNK)
    tril16 = _ones_where(tril)
    triu16 = _ones_where(triu)

    alog_row = pcol_ref[0, 0:1, :]
    dtb_row = pcol_ref[0, 1:2, :]
    neg_decay_rate = -jnp.exp(alog_row)

    def gate_cols(c, carry):
        rows = pl.ds(pl.multiple_of(c * CHUNK, CHUNK), CHUNK)
        ga = gc_ref[rows, :]
        g = neg_decay_rate * _softplus(ga + dtb_row)
        gcol_ref[rows, :] = _cumsum_rows(g, tril16)
        bcol_ref[rows, :] = _sigmoid(ga)
        return carry

    lax.fori_loop(0, nch, gate_cols, 0, unroll=4)

    for i in range(HB):
        hh = grp * HB + i
        a_r = gr_ref[hh, 0]
        alog = jnp.full((nch, CHUNK), prow_ref[0, hh], F32)
        g_r = -jnp.exp(alog) * _softplus(a_r + prow_ref[1, hh])
        grow_ref[i] = _cumsum_lanes(g_r, triu16)

    tile = 256
    xp_ref[0:8, :] = jnp.zeros((8, LANES), F32)
    jobs = []
    for i in range(HB):
        cs = slice(i * LANES, (i + 1) * LANES)
        jobs.append((q_ref, cwq_ref, qn_ref, cs, HEAD_DV ** -0.5))
        jobs.append((k_ref, cwk_ref, kn_ref, cs, 1.0))
        jobs.append((v_ref, cwv_ref, vn_ref, cs, None))
    for src, cw_ref, dst, cs, scale in jobs:
        xp_ref[8:seq + 8, :] = src[:, cs]
        cw = cw_ref[:, cs]

        def conv_tile(t, carry, dst=dst, cs=cs, cw=cw, scale=scale):
            r0 = pl.multiple_of(t * tile, tile)
            y = cw[3:4, :] * xp_ref[pl.ds(r0 + 8, tile), :]
            for j in range(GDN_CONV - 1):
                y = y + cw[j:j + 1, :] * xp_ref[pl.ds(r0 + 5 + j, tile), :]
            y = y * _sigmoid(y)
            if scale is not None:
                y = y * lax.rsqrt(jnp.sum(y * y, axis=-1, keepdims=True) + NORM_EPS)
                if scale != 1.0:
                    y = y * scale
            dst[pl.ds(r0, tile), cs] = y
            return carry

        lax.fori_loop(0, seq // tile, conv_tile, 0, unroll=2)

    chunks_per_iter = 8

    def intra(cc, side_work=()):
        pending = list(side_work)

        def emit_side():
            if pending:
                pending.pop(0)()

        items = []
        for u in range(chunks_per_iter):
            c = cc * chunks_per_iter + u
            rows = pl.ds(pl.multiple_of(c * CHUNK, CHUNK), CHUNK)
            for i in range(HB):
                items.append((c, rows, i, slice(i * LANES, (i + 1) * LANES)))
        q = [qn_ref[rows, cs] for _, rows, _, cs in items]
        k = [kn_ref[rows, cs] for _, rows, _, cs in items]
        v = [vn_ref[rows, cs] for _, rows, _, cs in items]
        beta = [bcol_ref[rows, HB + i:HB + i + 1] for _, rows, i, _ in items]
        gc = [gcol_ref[rows, i:i + 1] for _, rows, i, _ in items]
        gr = [grow_ref[i, pl.ds(c, 1), :] for c, _, i, _ in items]
        decay = [jnp.where(tril, jnp.exp(jnp.where(tril, a - b, 0.0)), 0.0) for a, b in zip(gc, gr)]
        kb = [a * b for a, b in zip(k, beta)]
        k16 = [a.astype(BF16) for a in k]
        g_kk = [_dot_nt(a.astype(BF16), b) for a, b in zip(kb, k16)]
        g_qk = [_dot_nt(a.astype(BF16), b) for a, b in zip(q, k16)]
        emit_side()
        a_mat = [jnp.where(strict, a * d, 0.0) for a, d in zip(g_kk, decay)]
        inv_m1 = _unit_lower_inverse_minus_identity(a_mat, emit_side)
        egc = [jnp.exp(a) for a in gc]
        rhs = [jnp.concatenate([a * b, c_ * e], axis=1)
               for a, b, c_, e in zip(v, beta, kb, egc)]
        sol = [a + _dot3(m, a) for m, a in zip(inv_m1, rhs)]
        emit_side()
        sol16 = [a.astype(BF16) for a in sol]
        kst16 = [(a * jnp.exp(g_[CHUNK - 1:CHUNK, :] - g_)).astype(BF16) for a, g_ in zip(k, gc)]
        att16 = [jnp.where(tril, a * d, 0.0).astype(BF16) for a, d in zip(g_qk, decay)]
        kst_sol = [_dot_tn(a, b) for a, b in zip(kst16, sol16)]
        att_sol = [_dot(a, b) for a, b in zip(att16, sol16)]
        while pending:
            emit_side()
        for n, (c, rows, i, cs) in enumerate(items):
            t_rows = pl.ds(pl.multiple_of(c * fold_rows, CHUNK), HEAD_DV)
            q_rows = pl.ds(pl.multiple_of(c * fold_rows + HEAD_DV, CHUNK), CHUNK)
            fold_ref[t_rows, cs] = (-kst_sol[n][:, LANES:]).astype(BF16)
            fold_ref[q_rows, cs] = (q[n] * egc[n] - att_sol[n][:, LANES:]).astype(BF16)
            sadd_ref[pl.ds(pl.multiple_of(c * HEAD_DV, HEAD_DV), HEAD_DV), cs] = kst_sol[n][:, :LANES]
            oadd_ref[rows, cs] = att_sol[n][:, :LANES]

    ng = ng_ref[...]

    def step(c, states):
        r0 = pl.multiple_of(c * CHUNK, CHUNK)
        rows = pl.ds(r0, CHUNK)
        heads = range(HB)
        cols = [slice(i * LANES, (i + 1) * LANES) for i in heads]
        f_rows = pl.ds(pl.multiple_of(c * fold_rows, CHUNK), fold_rows)
        s_rows = pl.ds(pl.multiple_of(c * HEAD_DV, HEAD_DV), HEAD_DV)
        r = [_dot(fold_ref[f_rows, cols[i]], states[i].astype(BF16)) for i in heads]
        new_states = []
        for i in heads:
            egl = jnp.exp(gcol_ref[pl.ds(r0 + CHUNK - 1, 1), i:i + 1])
            new_states.append(states[i] * egl + r[i][:HEAD_DV] + sadd_ref[s_rows, cols[i]])
            z = z_ref[rows, cols[i]]
            o = r[i][HEAD_DV:] + oadd_ref[rows, cols[i]]
            o_ref[rows, cols[i]] = (_rms(o, ng) * (z * _sigmoid(z))).astype(o_ref.dtype)
        return tuple(new_states)

    n_groups = nch // chunks_per_iter

    def fused(cc, states):
        box = [states]

        def scan_piece(k):
            def run():
                box[0] = step((cc - 1) * chunks_per_iter + k, box[0])
            return run

        intra(cc, [scan_piece(k) for k in range(chunks_per_iter)])
        return box[0]

    init = tuple(jnp.zeros((HEAD_DV, HEAD_DV), F32) for _ in range(HB))
    intra(jnp.int32(0))
    states = lax.fori_loop(1, n_groups, fused, init)
    lax.fori_loop((n_groups - 1) * chunks_per_iter, nch, step, states)


def _gdn_core(proj, ga, gat, conv_w, a_log, dt_bias, norm_g, *, batch, seq):
    m = batch * seq
    nch = seq // CHUNK
    wb = HB * LANES
    g_ = GROUPS
    gat4 = gat.reshape(2 * HEADS, batch, nch, CHUNK)
    pcol = _group_rows(a_log, dt_bias)
    prow = jnp.stack([a_log, dt_bias], axis=0)
    kernel = functools.partial(_gdn_kernel, seq=seq)
    return pl.pallas_call(
        kernel, grid=(batch, g_),
        in_specs=[
            pl.BlockSpec((seq, wb), lambda b, g: (b, g)),
            pl.BlockSpec((seq, wb), lambda b, g: (b, g_ + g)),
            pl.BlockSpec((seq, wb), lambda b, g: (b, 2 * g_ + g)),
            pl.BlockSpec((seq, wb), lambda b, g: (b, 3 * g_ + g)),
            pl.BlockSpec((seq, LANES), lambda b, g: (b, g)),
            pl.BlockSpec((2 * HEADS, 1, nch, CHUNK), lambda b, g: (0, b, 0, 0)),
            pl.BlockSpec((GDN_CONV, wb), lambda b, g: (0, g)),
            pl.BlockSpec((GDN_CONV, wb), lambda b, g: (0, g_ + g)),
            pl.BlockSpec((GDN_CONV, wb), lambda b, g: (0, 2 * g_ + g)),
            pl.BlockSpec((1, 2, LANES), lambda b, g: (g, 0, 0)),
            pl.BlockSpec(memory_space=pltpu.SMEM),
            pl.BlockSpec((1, LANES), lambda b, g: (0, 0)),
        ],
        out_specs=pl.BlockSpec((seq, wb), lambda b, g: (b, g)),
        out_shape=jax.ShapeDtypeStruct((m, HEADS * HEAD_DV), BF16),
        scratch_shapes=[
            pltpu.VMEM((seq + 8, LANES), F32),
            pltpu.VMEM((seq, wb), F32),
            pltpu.VMEM((seq, wb), F32),
            pltpu.VMEM((seq, wb), F32),
            pltpu.VMEM((seq, LANES), F32),
            pltpu.VMEM((seq, LANES), F32),
            pltpu.VMEM((HB, nch, CHUNK), F32),
            pltpu.VMEM((nch * (HEAD_DV + CHUNK), wb), BF16),
            pltpu.VMEM((nch * HEAD_DV, wb), F32),
            pltpu.VMEM((seq, wb), F32),
        ],
        compiler_params=pltpu.CompilerParams(dimension_semantics=("parallel", "arbitrary"),
                                             vmem_limit_bytes=VMEM_LIMIT),
        name="gdn_core",
    )(proj, proj, proj, proj, ga, gat4, conv_w, conv_w, conv_w, pcol, prow,
      norm_g.reshape(1, HEAD_DV))


def _mlstm_kernel(q_ref, k_ref, v_ref, og_ref, gc_ref, gr_ref, pcol_ref, prow_ref, ng_ref,
                  o_ref, icol_ref, bcol_ref, drow_ref, c_ref, *, seq):
    nch = seq // CHUNK
    dk = HEAD_DV // 2
    grp = pl.program_id(1)
    tril, _, triu = _tri_masks(CHUNK)
    tril16 = _ones_where(tril)
    triu16 = _ones_where(triu)
    bias_row = pcol_ref[0, 0:1, :]

    def gate_cols(c, carry):
        rows = pl.ds(pl.multiple_of(c * CHUNK, CHUNK), CHUNK)
        pre = _soft_cap(gc_ref[rows, :] + bias_row)
        icol_ref[rows, :] = pre
        bcol_ref[rows, :] = _cumsum_rows(-_softplus(-pre), tril16)
        return carry

    lax.fori_loop(0, nch, gate_cols, 0, unroll=4)

    for i in range(HB):
        hh = grp * HB + i
        i_r = _soft_cap(gr_ref[hh, 0] + prow_ref[0, hh])
        f_r = _soft_cap(gr_ref[HEADS + hh, 0] + prow_ref[1, hh])
        drow_ref[i] = i_r - _cumsum_lanes(-_softplus(-f_r), triu16)

    lane = lax.broadcasted_iota(jnp.int32, (CHUNK, LANES), 1)
    ones_col = _ones_where(lane == 0)
    c_ref[...] = jnp.zeros(c_ref.shape, F32)

    chunks_per_iter = 4
    head_lanes = [(lane >= i * dk) & (lane < (i + 1) * dk) for i in range(HB)]
    cols = [slice(i * LANES, (i + 1) * LANES) for i in range(HB)]

    def step(cc, m_states):
        m_st = list(m_states)
        items = [(u, i) for u in range(chunks_per_iter) for i in range(HB)]
        chunk = [cc * chunks_per_iter + u for u in range(chunks_per_iter)]
        rows = [pl.ds(pl.multiple_of(c * CHUNK, CHUNK), CHUNK) for c in chunk]
        qp = [q_ref[r, :] for r in rows]
        kp = [k_ref[r, :] for r in rows]
        kp16 = [x.astype(BF16) for x in kp]
        bcum = [bcol_ref[rows[u], HB + i:HB + i + 1] for u, i in items]
        ig = [icol_ref[rows[u], i:i + 1] for u, i in items]
        dlog = [jnp.where(tril, b + drow_ref[i, pl.ds(chunk[u], 1), :], -jnp.inf)
                for b, (u, i) in zip(bcum, items)]
        m_intra = [jnp.max(d, axis=-1, keepdims=True) for d in dlog]
        q16 = [(jnp.where(head_lanes[i], qp[u], 0.0) * dk ** -0.5).astype(BF16) for u, i in items]
        s_raw = [_dot_nt(q, kp16[u]) for q, (u, i) in zip(q16, items)]
        g = [b[CHUNK - 1:CHUNK, :] for b in bcum]
        a = [g_ - b + i_ for g_, b, i_ in zip(g, bcum, ig)]
        a_max = [jnp.max(x, axis=0, keepdims=True) for x in a]
        m_out, w_inter, w_old, m_new = [], [], [], []
        for n, (u, i) in enumerate(items):
            m_inter = bcum[n] + m_st[i]
            m_out.append(jnp.maximum(m_inter, m_intra[n]))
            w_inter.append(jnp.exp(m_inter - m_out[n]))
            m_new.append(jnp.maximum(g[n] + m_st[i], a_max[n]))
            w_old.append(jnp.exp(g[n] + m_st[i] - m_new[n]))
            m_st[i] = m_new[n]
        vaug = [jnp.concatenate([v_ref[rows[u], cols[i]].astype(BF16), ones_col], axis=1)
                for u, i in items]
        s16 = [(s * jnp.exp(d - m)).astype(BF16) for s, d, m in zip(s_raw, dlog, m_out)]
        sv = [_dot(s, v) for s, v in zip(s16, vaug)]
        kw = [(jnp.where(head_lanes[i], kp[u], 0.0) * jnp.exp(a[n] - m_new[n])).astype(BF16)
              for n, (u, i) in enumerate(items)]
        kv = [_dot_tn(k_, v) for k_, v in zip(kw, vaug)]
        cst = [c_ref[i] for i in range(HB)]
        for n, (u, i) in enumerate(items):
            nd = w_inter[n] * _dot(q16[n], cst[i].astype(BF16)) + sv[n]
            den = nd[:, LANES:LANES + 1]
            h = nd[:, :LANES] * (1.0 / jnp.maximum(jnp.abs(den), jnp.exp(-m_out[n])))
            cst[i] = w_old[n] * cst[i] + kv[n]
            hn = _rms(h, ng_ref[:, cols[i]])
            o_ref[rows[u], cols[i]] = (_sigmoid(og_ref[rows[u], cols[i]]) * hn).astype(o_ref.dtype)
        for i in range(HB):
            c_ref[i] = cst[i]
        return tuple(m_st)

    lax.fori_loop(0, nch // chunks_per_iter, step,
                  tuple(jnp.zeros((1, 1), F32) for _ in range(HB)))


def _mlstm_core(proj, ga, gat, gate_b, norm_g, *, batch, seq):
    m = batch * seq
    nch = seq // CHUNK
    wb = HB * LANES
    g_ = GROUPS
    gat4 = gat.reshape(2 * HEADS, batch, nch, CHUNK)
    pad = jnp.zeros((GROUPS, LANES - 2 * HB), F32)
    pcol = jnp.concatenate([gate_b[0].reshape(GROUPS, HB), gate_b[1].reshape(GROUPS, HB), pad],
                           axis=1).reshape(GROUPS, 1, LANES)
    kernel = functools.partial(_mlstm_kernel, seq=seq)
    return pl.pallas_call(
        kernel, grid=(batch, g_),
        in_specs=[
            pl.BlockSpec((seq, LANES), lambda b, g: (b, g)),
            pl.BlockSpec((seq, LANES), lambda b, g: (b, g_ + g)),
            pl.BlockSpec((seq, wb), lambda b, g: (b, g_ + g)),
            pl.BlockSpec((seq, wb), lambda b, g: (b, 2 * g_ + g)),
            pl.BlockSpec((seq, LANES), lambda b, g: (b, g)),
            pl.BlockSpec((2 * HEADS, 1, nch, CHUNK), lambda b, g: (0, b, 0, 0)),
            pl.BlockSpec((1, 1, LANES), lambda b, g: (g, 0, 0)),
            pl.BlockSpec(memory_space=pltpu.SMEM),
            pl.BlockSpec((1, wb), lambda b, g: (0, g)),
        ],
        out_specs=pl.BlockSpec((seq, wb), lambda b, g: (b, g)),
        out_shape=jax.ShapeDtypeStruct((m, HEADS * HEAD_DV), BF16),
        scratch_shapes=[
            pltpu.VMEM((seq, LANES), F32),
            pltpu.VMEM((seq, LANES), F32),
            pltpu.VMEM((HB, nch, CHUNK), F32),
            pltpu.VMEM((HB, LANES, 2 * LANES), F32),
        ],
        compiler_params=pltpu.CompilerParams(dimension_semantics=("parallel", "arbitrary"),
                                             vmem_limit_bytes=VMEM_LIMIT),
        name="mlstm_core",
    )(proj, proj, proj, proj, ga, gat4, pcol, gate_b, norm_g.reshape(1, HEADS * HEAD_DV))


def _rope(x, cos, sin_lo, sin_hi):
    half = ROT_DIMS // 2
    return (x * cos + pltpu.roll(x, LANES - half, axis=1) * sin_lo
            + pltpu.roll(x, half, axis=1) * sin_hi)


def _diff_kernel(q_ref, k_ref, v_ref, tq_ref, tk_ref, lam_ref, ng_ref, o_ref,
                 kr_ref, vt_ref, q2_ref, m_ref, acc_ref, *, seq, tq, dh, lambda_init):
    qi = pl.program_id(2)
    tk = tq

    @pl.when(qi == 0)
    def _():
        def prep_tile(t, carry):
            rows = pl.ds(pl.multiple_of(t * tk, tk), tk)
            for d in range(dh):
                cs = slice(d * LANES, (d + 1) * LANES)
                kr_ref[d, rows, :] = _rope(k_ref[rows, cs], tk_ref[0, rows, :], tk_ref[1, rows, :],
                                           tk_ref[2, rows, :]).astype(BF16)
                vt_ref[d, 0:LANES, rows] = v_ref[rows, cs].T.astype(BF16)
                vt_ref[d, LANES:LANES + ONES_ROWS, rows] = _ones_where(
                    lax.broadcasted_iota(jnp.int32, (ONES_ROWS, tk), 0) == 0)
            return carry
        lax.fori_loop(0, seq // tk, prep_tile, 0)

    lane = lax.broadcasted_iota(jnp.int32, (tq, LANES), 1)
    for d in range(dh):
        cs = slice(d * LANES, (d + 1) * LANES)
        q = _rope(q_ref[:, cs], tq_ref[0], tq_ref[1], tq_ref[2]) * (DIFF_D ** -0.5 * LOG2_E)
        q2_ref[d] = jnp.concatenate(
            [jnp.where(lane < DIFF_D, q, 0.0), jnp.where(lane >= DIFF_D, q, 0.0)],
            axis=0).astype(BF16)
    m_ref[...] = jnp.full(m_ref.shape, -jnp.inf, F32)
    acc_ref[...] = jnp.zeros(acc_ref.shape, F32)

    def kv_block(j, masked):
        rows = pl.ds(pl.multiple_of(j * tk, tk), tk)
        st = [_dot_nt(kr_ref[d, rows, :], q2_ref[d]) for d in range(dh)]
        if masked:
            kpos = lax.broadcasted_iota(jnp.int32, (tk, 2 * tq), 0)
            qpos = lax.broadcasted_iota(jnp.int32, (tk, 2 * tq), 1)
            qpos = jnp.where(qpos >= tq, qpos - tq, qpos)
            shift = CHUNK.bit_length() - 1
            keep = jnp.right_shift(kpos, shift) <= jnp.right_shift(qpos, shift)
            st = [jnp.where(keep, s, -jnp.inf) for s in st]
        m_prev = [m_ref[d] for d in range(dh)]
        m_new = [jnp.maximum(m, jnp.max(s, axis=0, keepdims=True))
                 for m, s in zip(m_prev, st)]
        p16 = [jnp.exp2(s - m).astype(BF16) for s, m in zip(st, m_new)]
        pv = [_dot(vt_ref[d, :, rows], p16[d]) for d in range(dh)]
        for d in range(dh):
            acc_ref[d] = jnp.exp2(m_prev[d] - m_new[d]) * acc_ref[d] + pv[d]
            m_ref[d] = m_new[d]

    def body(j, carry):
        kv_block(j, False)
        return carry

    lax.fori_loop(0, qi, body, 0)
    kv_block(qi, True)

    lp = lam_ref[...]
    lam = (jnp.exp(jnp.sum(lp[0:1] * lp[1:2], axis=-1, keepdims=True))
           - jnp.exp(jnp.sum(lp[2:3] * lp[3:4], axis=-1, keepdims=True)) + lambda_init)
    for d in range(dh):
        acc = acc_ref[d]
        out_t = acc[:LANES] * (1.0 / acc[LANES:LANES + 1])
        o = (out_t[:, :tq] - lam * out_t[:, tq:]).T
        o_ref[:, d * LANES:(d + 1) * LANES] = (
            _rms(o, ng_ref[...]) * (1.0 - lambda_init)).astype(o_ref.dtype)


def _rope_tables(seq):
    half = ROT_DIMS // 2
    inv_freq = ROPE_THETA ** (-jnp.arange(0, ROT_DIMS, 2, dtype=F32) / ROT_DIMS)
    ang = jnp.arange(seq, dtype=F32)[:, None] * inv_freq[None, :]
    cos, sin = jnp.cos(ang), jnp.sin(ang)
    one = jnp.ones((seq, DIFF_D - ROT_DIMS), F32)
    zero = jnp.zeros((seq, DIFF_D - ROT_DIMS), F32)
    zh = jnp.zeros((seq, half), F32)
    cos_c = jnp.concatenate([cos, cos, one], axis=1)
    lo_c = jnp.concatenate([-sin, zh, zero], axis=1)
    hi_c = jnp.concatenate([zh, sin, zero], axis=1)
    rep = lambda t: jnp.concatenate([t, t], axis=1)
    return jnp.stack([rep(cos_c), rep(lo_c), rep(hi_c)], axis=0)


def _diff_core(proj, tables, lam_p, norm_g, lambda_init, *, batch, seq, tq=256, dh=4):
    m = batch * seq
    nq = seq // tq
    hg = HEADS // dh
    wb = dh * LANES
    kernel = functools.partial(_diff_kernel, seq=seq, tq=tq, dh=dh, lambda_init=lambda_init)
    return pl.pallas_call(
        kernel, grid=(batch, hg, nq),
        in_specs=[
            pl.BlockSpec((tq, wb), lambda b, h, i: (b * nq + i, h)),
            pl.BlockSpec((seq, wb), lambda b, h, i: (b, hg + h)),
            pl.BlockSpec((seq, wb), lambda b, h, i: (b, 2 * hg + h)),
            pl.BlockSpec((3, tq, LANES), lambda b, h, i: (0, i, 0)),
            pl.BlockSpec((3, seq, LANES), lambda b, h, i: (0, 0, 0)),
            pl.BlockSpec((4, DIFF_D), lambda b, h, i: (0, 0)),
            pl.BlockSpec((1, LANES), lambda b, h, i: (0, 0)),
        ],
        out_specs=pl.BlockSpec((tq, wb), lambda b, h, i: (b * nq + i, h)),
        out_shape=jax.ShapeDtypeStruct((m, HEADS * 2 * DIFF_D), BF16),
        scratch_shapes=[
            pltpu.VMEM((dh, seq, LANES), BF16),
            pltpu.VMEM((dh, LANES + ONES_ROWS, seq), BF16),
            pltpu.VMEM((dh, 2 * tq, LANES), BF16),
            pltpu.VMEM((dh, 1, 2 * tq), F32),
            pltpu.VMEM((dh, LANES + ONES_ROWS, 2 * tq), F32),
        ],
        compiler_params=pltpu.CompilerParams(
            dimension_semantics=("parallel", "parallel", "arbitrary"),
            vmem_limit_bytes=VMEM_LIMIT),
        name="diff_core",
    )(proj, proj, proj, tables, tables, lam_p, norm_g.reshape(1, 2 * DIFF_D))


def _out_proj_kernel(o_ref, w_ref, g_ref, x_ref, y_ref):
    y_ref[...] = x_ref[...] + _rms(_dot(o_ref[...], w_ref[...]), g_ref[...])


def _out_proj(o16, w16, g, x2, *, tm=1024):
    m, d = x2.shape
    return pl.pallas_call(
        _out_proj_kernel, grid=(m // tm,),
        in_specs=[pl.BlockSpec((tm, d), lambda i: (i, 0)),
                  pl.BlockSpec((d, d), lambda i: (0, 0)),
                  pl.BlockSpec((1, d), lambda i: (0, 0)),
                  pl.BlockSpec((tm, d), lambda i: (i, 0))],
        out_specs=pl.BlockSpec((tm, d), lambda i: (i, 0)),
        out_shape=jax.ShapeDtypeStruct((m, d), F32),
        compiler_params=pltpu.CompilerParams(dimension_semantics=("parallel",),
                                             vmem_limit_bytes=VMEM_LIMIT),
        name="out_proj",
    )(o16, w16, g, x2)


def _mlp_kernel(x_ref, g2_ref, w1_ref, w2_ref, g3_ref, y_ref, h_ref, acc_ref):
    j = pl.program_id(1)

    @pl.when(j == 0)
    def _():
        h_ref[...] = _rms(x_ref[...], g2_ref[...]).astype(BF16)
        acc_ref[...] = jnp.zeros(acc_ref.shape, F32)

    a = jnp.maximum(_dot(h_ref[...], w1_ref[...]), 0.0)
    acc_ref[...] += _dot((a * a).astype(BF16), w2_ref[...])

    @pl.when(j == pl.num_programs(1) - 1)
    def _():
        y_ref[...] = x_ref[...] + _rms(acc_ref[...], g3_ref[...])


def _mlp(x2, g2, w1_16, w2_16, g3, *, tm=1024, tf=1024):
    m, d = x2.shape
    f = w1_16.shape[1]
    return pl.pallas_call(
        _mlp_kernel, grid=(m // tm, f // tf),
        in_specs=[pl.BlockSpec((tm, d), lambda i, j: (i, 0)),
                  pl.BlockSpec((1, d), lambda i, j: (0, 0)),
                  pl.BlockSpec((d, tf), lambda i, j: (0, j)),
                  pl.BlockSpec((tf, d), lambda i, j: (j, 0)),
                  pl.BlockSpec((1, d), lambda i, j: (0, 0))],
        out_specs=pl.BlockSpec((tm, d), lambda i, j: (i, 0)),
        out_shape=jax.ShapeDtypeStruct((m, d), F32),
        scratch_shapes=[pltpu.VMEM((tm, d), BF16), pltpu.VMEM((tm, d), F32)],
        compiler_params=pltpu.CompilerParams(dimension_semantics=("parallel", "arbitrary"),
                                             vmem_limit_bytes=VMEM_LIMIT),
        name="mlp",
    )(x2, g2, w1_16, w2_16, g3)


def kernel(x, norm_g, mlp_w1, mlp_w2, gdn_w_in, gdn_conv, gdn_a_log, gdn_dt_bias, gdn_norm_g,
           gdn_w_out, mlstm_w_in, mlstm_gate_b, mlstm_norm_g, mlstm_w_out, diff_w_in,
           diff_lambda, diff_norm_g, diff_w_out):
    batch, seq, d = x.shape
    depth = norm_g.shape[0]
    x2 = x.reshape(batch * seq, d)
    tables = _rope_tables(seq)
    row = lambda v: v.reshape(1, -1)
    for i in range(depth):
        kind, j = i % 3, i // 3
        g_pre = row(norm_g[i, 0])
        if kind == 0:
            w_in = gdn_w_in[j]
            n_main = 4 * HEADS * HEAD_DV
            wc, wr = _gate_weights(w_in[:, n_main:n_main + HEADS], w_in[:, n_main + HEADS:])
            proj, ga, gat = _in_proj(x2, g_pre, w_in[:, :n_main].astype(BF16), wc, wr)
            o16 = _gdn_core(proj, ga, gat, gdn_conv[j], gdn_a_log[j], gdn_dt_bias[j],
                            gdn_norm_g[j], batch=batch, seq=seq)
            w_out = gdn_w_out[j]
        elif kind == 1:
            w_in = mlstm_w_in[j]
            n_main = 3 * HEADS * HEAD_DV
            wc, wr = _gate_weights(w_in[:, n_main:n_main + HEADS], w_in[:, n_main + HEADS:])
            proj, ga, gat = _in_proj(x2, g_pre, w_in[:, :n_main].astype(BF16), wc, wr)
            o16 = _mlstm_core(proj, ga, gat, mlstm_gate_b[j], mlstm_norm_g[j],
                              batch=batch, seq=seq)
            w_out = mlstm_w_out[j]
        else:
            lambda_init = 0.8 - 0.6 * math.exp(-0.3 * i)
            proj = _in_proj(x2, g_pre, diff_w_in[j].astype(BF16))
            o16 = _diff_core(proj, tables, diff_lambda[j], diff_norm_g[j], lambda_init,
                             batch=batch, seq=seq)
            w_out = diff_w_out[j]
        x2 = _out_proj(o16, w_out.astype(BF16), row(norm_g[i, 1]), x2)
        x2 = _mlp(x2, row(norm_g[i, 2]), mlp_w1[i].astype(BF16), mlp_w2[i].astype(BF16),
                  row(norm_g[i, 3]))
    return x2.reshape(batch, seq, d)
```

```python
import functools
import math

import jax
import jax.numpy as jnp
from jax import lax
from jax.experimental import pallas as pl
from jax.experimental.pallas import tpu as pltpu

F32 = jnp.float32
BF16 = jnp.bfloat16

D_MODEL = 1024
D_FF = 4 * D_MODEL
HEADS = 8
HEAD_DV = 128
CHUNK = 64
NORM_EPS = 1e-6
GATE_CAP = 15.0
GDN_CONV = 4
ROPE_THETA = 500000.0
DIFF_D = 64
ROT_DIMS = DIFF_D // 4
LANES = 128
ONES_ROWS = 16
LOG2_E = math.log2(math.e)
HB = 2
GROUPS = HEADS // HB
VMEM_LIMIT = 56 * 1024 * 1024

_NT = (((1,), (1,)), ((), ()))
_TN = (((0,), (0,)), ((), ()))


def _dot(a, b):
    return jnp.dot(a, b, preferred_element_type=F32)


def _dot_nt(a, b):
    return lax.dot_general(a, b, _NT, preferred_element_type=F32)


def _dot_tn(a, b):
    return lax.dot_general(a, b, _TN, preferred_element_type=F32)


def _rms(x, g):
    ms = jnp.mean(x * x, axis=-1, keepdims=True)
    return x * lax.rsqrt(ms + NORM_EPS) * g


def _sigmoid(x):
    return 1.0 / (1.0 + jnp.exp(-x))


def _softplus(x):
    return jnp.maximum(x, 0.0) + jnp.log1p(jnp.exp(-jnp.abs(x)))


def _soft_cap(x):
    return GATE_CAP * jnp.tanh(x / GATE_CAP)


def _split3(x):
    x1 = x.astype(BF16)
    r1 = x - x1.astype(F32)
    x2 = r1.astype(BF16)
    r2 = r1 - x2.astype(F32)
    return x1, x2, r2.astype(BF16)


def _cumsum_rows(x, tril16):
    a, b, c = _split3(x)
    return _dot(tril16, a) + _dot(tril16, b) + _dot(tril16, c)


def _cumsum_lanes(x, triu16):
    a, b, c = _split3(x)
    return _dot(a, triu16) + _dot(b, triu16) + _dot(c, triu16)


def _split2(x):
    hi = x.astype(BF16)
    return hi, (x - hi.astype(F32)).astype(BF16)


def _dot3(a, b):
    ah, al = _split2(a)
    bh, bl = _split2(b)
    return _dot(ah, bh) + _dot(ah, bl) + _dot(al, bh)


def _dot2(a, b):
    ah = a.astype(BF16)
    bh, bl = _split2(b)
    return _dot(ah, bh) + _dot(ah, bl)


def _unit_lower_inverse_minus_identity(a, between_levels=None):
    n = a[0].shape[0]
    r = lax.broadcasted_iota(jnp.int32, (n, n), 0)
    c = lax.broadcasted_iota(jnp.int32, (n, n), 1)
    pair = jnp.right_shift(r, 1) == jnp.right_shift(c, 1)
    a = [x.astype(BF16).astype(F32) for x in a]
    inv_m1 = [-jnp.where(pair, x, 0.0) for x in a]
    s = 2
    while s < n:
        sh = s.bit_length() - 1
        merge = ((jnp.right_shift(r, sh + 1) == jnp.right_shift(c, sh + 1))
                 & (jnp.right_shift(r, sh) != jnp.right_shift(c, sh)))
        off = [jnp.where(merge, x, 0.0) for x in a]
        t = [_dot2(o, m) for o, m in zip(off, inv_m1)]
        y = [o + x for o, x in zip(off, t)]
        t = [_dot3(m, x) for m, x in zip(inv_m1, y)]
        inv_m1 = [m - (x + z) for m, x, z in zip(inv_m1, y, t)]
        if between_levels is not None:
            between_levels()
        s *= 2
    return inv_m1


def _tri_masks(n):
    r = lax.broadcasted_iota(jnp.int32, (n, n), 0)
    c = lax.broadcasted_iota(jnp.int32, (n, n), 1)
    return r >= c, r > c, r <= c


def _ones_where(mask):
    return jnp.where(mask, 1.0, 0.0).astype(BF16)


def _in_proj_kernel(x_ref, g_ref, w_ref, o_ref, h_ref):
    @pl.when(pl.program_id(1) == 0)
    def _():
        h_ref[...] = _rms(x_ref[...], g_ref[...]).astype(BF16)

    o_ref[...] = _dot(h_ref[...], w_ref[...]).astype(o_ref.dtype)


def _in_proj_gates_kernel(x_ref, g_ref, w_ref, wc_ref, wr_ref, o_ref, oc_ref, or_ref, h_ref):
    @pl.when(pl.program_id(1) == 0)
    def _():
        h = _rms(x_ref[...], g_ref[...]).astype(BF16)
        h_ref[...] = h
        oc_ref[...] = _dot(h, wc_ref[...])
        or_ref[...] = _dot_nt(wr_ref[...], h)

    o_ref[...] = _dot(h_ref[...], w_ref[...]).astype(o_ref.dtype)


def _in_proj(x2, g, w16, wc16=None, wr16=None, *, tm=512):
    m, d = x2.shape
    n = w16.shape[1]
    tn = n
    grid = (m // tm, n // tn)
    x_spec = pl.BlockSpec((tm, d), lambda i, j: (i, 0))
    g_spec = pl.BlockSpec((1, d), lambda i, j: (0, 0))
    w_spec = pl.BlockSpec((d, tn), lambda i, j: (0, j))
    o_spec = pl.BlockSpec((tm, tn), lambda i, j: (i, j))
    params = pltpu.CompilerParams(dimension_semantics=("parallel", "arbitrary"),
                                  vmem_limit_bytes=VMEM_LIMIT)
    if wc16 is None:
        return pl.pallas_call(
            _in_proj_kernel, grid=grid,
            in_specs=[x_spec, g_spec, w_spec], out_specs=o_spec,
            out_shape=jax.ShapeDtypeStruct((m, n), F32),
            scratch_shapes=[pltpu.VMEM((tm, d), BF16)],
            compiler_params=params, name="in_proj",
        )(x2, g, w16)
    nc = wc16.shape[1]
    nr = wr16.shape[0]
    return pl.pallas_call(
        _in_proj_gates_kernel, grid=grid,
        in_specs=[x_spec, g_spec, w_spec,
                  pl.BlockSpec((d, nc), lambda i, j: (0, 0)),
                  pl.BlockSpec((nr, d), lambda i, j: (0, 0))],
        out_specs=[o_spec,
                   pl.BlockSpec((tm, nc), lambda i, j: (i, 0)),
                   pl.BlockSpec((nr, tm), lambda i, j: (0, i))],
        out_shape=[jax.ShapeDtypeStruct((m, n), F32),
                   jax.ShapeDtypeStruct((m, nc), F32),
                   jax.ShapeDtypeStruct((nr, m), F32)],
        scratch_shapes=[pltpu.VMEM((tm, d), BF16)],
        compiler_params=params, name="in_proj_gates",
    )(x2, g, w16, wc16, wr16)


def _gate_weights(w_a, w_b):
    d = w_a.shape[0]
    a = w_a.reshape(d, GROUPS, HB)
    b = w_b.reshape(d, GROUPS, HB)
    pad = jnp.zeros((d, GROUPS, LANES - 2 * HB), w_a.dtype)
    wc = jnp.concatenate([a, b, pad], axis=2).reshape(d, GROUPS * LANES)
    wr = jnp.concatenate([w_a, w_b], axis=1).T
    return wc.astype(BF16), wr.astype(BF16)


def _group_rows(p_a, p_b):
    pad = jnp.zeros((GROUPS, LANES - HB), F32)
    ra = jnp.concatenate([p_a.reshape(GROUPS, HB), pad], axis=1)
    rb = jnp.concatenate([p_b.reshape(GROUPS, HB), pad], axis=1)
    return jnp.stack([ra, rb], axis=1)


def _gdn_kernel(q_ref, k_ref, v_ref, z_ref, gc_ref, gr_ref, cwq_ref, cwk_ref, cwv_ref,
                pcol_ref, prow_ref, ng_ref, o_ref,
                xp_ref, qn_ref, kn_ref, vn_ref, bcol_ref, gcol_ref, grow_ref,
                fold_ref, sadd_ref, oadd_ref, *, seq):
    nch = seq // CHUNK
    fold_rows = HEAD_DV + CHUNK
    grp = pl.program_id(1)
    tril, strict, triu = _tri_masks(CHUNK)
    tril16 = _ones_where(tril)
    triu16 = _ones_where(triu)

    alog_row = pcol_ref[0, 0:1, :]
    dtb_row = pcol_ref[0, 1:2, :]
    neg_decay_rate = -jnp.exp(alog_row)

    def gate_cols(c, carry):
        rows = pl.ds(pl.multiple_of(c * CHUNK, CHUNK), CHUNK)
        ga = gc_ref[rows, :]
        g = neg_decay_rate * _softplus(ga + dtb_row)
        gcol_ref[rows, :] = _cumsum_rows(g, tril16)
        bcol_ref[rows, :] = _sigmoid(ga)
        return carry

    lax.fori_loop(0, nch, gate_cols, 0, unroll=4)

    for i in range(HB):
        hh = grp * HB + i
        a_r = gr_ref[hh, 0]
        alog = jnp.full((nch, CHUNK), prow_ref[0, hh], F32)
        g_r = -jnp.exp(alog) * _softplus(a_r + prow_ref[1, hh])
        grow_ref[i] = _cumsum_lanes(g_r, triu16)

    tile = 256
    xp_ref[0:8, :] = jnp.zeros((8, LANES), F32)
    jobs = []
    for i in range(HB):
        cs = slice(i * LANES, (i + 1) * LANES)
        jobs.append((q_ref, cwq_ref, qn_ref, cs, HEAD_DV ** -0.5))
        jobs.append((k_ref, cwk_ref, kn_ref, cs, 1.0))
        jobs.append((v_ref, cwv_ref, vn_ref, cs, None))
    for src, cw_ref, dst, cs, scale in jobs:
        xp_ref[8:seq + 8, :] = src[:, cs]
        cw = cw_ref[:, cs]

        def conv_tile(t, carry, dst=dst, cs=cs, cw=cw, scale=scale):
            r0 = pl.multiple_of(t * tile, tile)
            y = cw[3:4, :] * xp_ref[pl.ds(r0 + 8, tile), :]
            for j in range(GDN_CONV - 1):
                y = y + cw[j:j + 1, :] * xp_ref[pl.ds(r0 + 5 + j, tile), :]
            half_y = 0.5 * y
            y = half_y + half_y * jnp.tanh(half_y)
            if scale is not None:
                y = y * lax.rsqrt(jnp.sum(y * y, axis=-1, keepdims=True) + NORM_EPS)
                if scale != 1.0:
                    y = y * scale
            dst[pl.ds(r0, tile), cs] = y
            return carry

        lax.fori_loop(0, seq // tile, conv_tile, 0, unroll=2)

    chunks_per_iter = 8

    def intra(cc, side_work=()):
        pending = list(side_work)

        def emit_side():
            if pending:
                pending.pop(0)()

        items = []
        for u in range(chunks_per_iter):
            c = cc * chunks_per_iter + u
            rows = pl.ds(pl.multiple_of(c * CHUNK, CHUNK), CHUNK)
            for i in range(HB):
                items.append((c, rows, i, slice(i * LANES, (i + 1) * LANES)))
        q = [qn_ref[rows, cs] for _, rows, _, cs in items]
        k = [kn_ref[rows, cs] for _, rows, _, cs in items]
        v = [vn_ref[rows, cs] for _, rows, _, cs in items]
        beta = [bcol_ref[rows, HB + i:HB + i + 1] for _, rows, i, _ in items]
        gc = [gcol_ref[rows, i:i + 1] for _, rows, i, _ in items]
        gr = [grow_ref[i, pl.ds(c, 1), :] for c, _, i, _ in items]
        decay = [jnp.where(tril, jnp.exp(jnp.where(tril, a - b, 0.0)), 0.0) for a, b in zip(gc, gr)]
        kb = [a * b for a, b in zip(k, beta)]
        k16 = [a.astype(BF16) for a in k]
        g_kk = [_dot_nt(a.astype(BF16), b) for a, b in zip(kb, k16)]
        g_qk = [_dot_nt(a.astype(BF16), b) for a, b in zip(q, k16)]
        emit_side()
        a_mat = [jnp.where(strict, a * d, 0.0) for a, d in zip(g_kk, decay)]
        inv_m1 = _unit_lower_inverse_minus_identity(a_mat, emit_side)
        egc = [jnp.exp(a) for a in gc]
        rhs = [jnp.concatenate([a * b, c_ * e], axis=1)
               for a, b, c_, e in zip(v, beta, kb, egc)]
        sol = [a + _dot3(m, a) for m, a in zip(inv_m1, rhs)]
        emit_side()
        sol16 = [a.astype(BF16) for a in sol]
        kst16 = [(a * jnp.exp(g_[CHUNK - 1:CHUNK, :] - g_)).astype(BF16) for a, g_ in zip(k, gc)]
        att16 = [jnp.where(tril, a * d, 0.0).astype(BF16) for a, d in zip(g_qk, decay)]
        kst_sol = [_dot_tn(a, b) for a, b in zip(kst16, sol16)]
        att_sol = [_dot(a, b) for a, b in zip(att16, sol16)]
        while pending:
            emit_side()
        for n, (c, rows, i, cs) in enumerate(items):
            t_rows = pl.ds(pl.multiple_of(c * fold_rows, CHUNK), HEAD_DV)
            q_rows = pl.ds(pl.multiple_of(c * fold_rows + HEAD_DV, CHUNK), CHUNK)
            fold_ref[t_rows, cs] = (-kst_sol[n][:, LANES:]).astype(BF16)
            fold_ref[q_rows, cs] = (q[n] * egc[n] - att_sol[n][:, LANES:]).astype(BF16)
            sadd_ref[pl.ds(pl.multiple_of(c * HEAD_DV, HEAD_DV), HEAD_DV), cs] = kst_sol[n][:, :LANES]
            oadd_ref[rows, cs] = att_sol[n][:, :LANES]

    ng = ng_ref[...]

    def step(c, states):
        r0 = pl.multiple_of(c * CHUNK, CHUNK)
        rows = pl.ds(r0, CHUNK)
        heads = range(HB)
        cols = [slice(i * LANES, (i + 1) * LANES) for i in heads]
        f_rows = pl.ds(pl.multiple_of(c * fold_rows, CHUNK), fold_rows)
        s_rows = pl.ds(pl.multiple_of(c * HEAD_DV, HEAD_DV), HEAD_DV)
        r = [_dot(fold_ref[f_rows, cols[i]], states[i].astype(BF16)) for i in heads]
        new_states = []
        for i in heads:
            egl = jnp.exp(gcol_ref[pl.ds(r0 + CHUNK - 1, 1), i:i + 1])
            new_states.append(states[i] * egl + r[i][:HEAD_DV] + sadd_ref[s_rows, cols[i]])
            z = z_ref[rows, cols[i]]
            o = r[i][HEAD_DV:] + oadd_ref[rows, cols[i]]
            o_ref[rows, cols[i]] = (_rms(o, ng) * (z * _sigmoid(z))).astype(o_ref.dtype)
        return tuple(new_states)

    n_groups = nch // chunks_per_iter

    def fused(cc, states):
        box = [states]

        def scan_piece(k):
            def run():
                box[0] = step((cc - 1) * chunks_per_iter + k, box[0])
            return run

        intra(cc, [scan_piece(k) for k in range(chunks_per_iter)])
        return box[0]

    init = tuple(jnp.zeros((HEAD_DV, HEAD_DV), F32) for _ in range(HB))
    intra(jnp.int32(0))
    states = lax.fori_loop(1, n_groups, fused, init)
    lax.fori_loop((n_groups - 1) * chunks_per_iter, nch, step, states)


def _gdn_core(proj, ga, gat, conv_w, a_log, dt_bias, norm_g, *, batch, seq):
    m = batch * seq
    nch = seq // CHUNK
    wb = HB * LANES
    g_ = GROUPS
    gat4 = gat.reshape(2 * HEADS, batch, nch, CHUNK)
    pcol = _group_rows(a_log, dt_bias)
    prow = jnp.stack([a_log, dt_bias], axis=0)
    kernel = functools.partial(_gdn_kernel, seq=seq)
    return pl.pallas_call(
        kernel, grid=(batch, g_),
        in_specs=[
            pl.BlockSpec((seq, wb), lambda b, g: (b, g)),
            pl.BlockSpec((seq, wb), lambda b, g: (b, g_ + g)),
            pl.BlockSpec((seq, wb), lambda b, g: (b, 2 * g_ + g)),
            pl.BlockSpec((seq, wb), lambda b, g: (b, 3 * g_ + g)),
            pl.BlockSpec((seq, LANES), lambda b, g: (b, g)),
            pl.BlockSpec((2 * HEADS, 1, nch, CHUNK), lambda b, g: (0, b, 0, 0)),
            pl.BlockSpec((GDN_CONV, wb), lambda b, g: (0, g)),
            pl.BlockSpec((GDN_CONV, wb), lambda b, g: (0, g_ + g)),
            pl.BlockSpec((GDN_CONV, wb), lambda b, g: (0, 2 * g_ + g)),
            pl.BlockSpec((1, 2, LANES), lambda b, g: (g, 0, 0)),
            pl.BlockSpec(memory_space=pltpu.SMEM),
            pl.BlockSpec((1, LANES), lambda b, g: (0, 0)),
        ],
        out_specs=pl.BlockSpec((seq, wb), lambda b, g: (b, g)),
        out_shape=jax.ShapeDtypeStruct((m, HEADS * HEAD_DV), BF16),
        scratch_shapes=[
            pltpu.VMEM((seq + 8, LANES), F32),
            pltpu.VMEM((seq, wb), F32),
            pltpu.VMEM((seq, wb), F32),
            pltpu.VMEM((seq, wb), F32),
            pltpu.VMEM((seq, LANES), F32),
            pltpu.VMEM((seq, LANES), F32),
            pltpu.VMEM((HB, nch, CHUNK), F32),
            pltpu.VMEM((nch * (HEAD_DV + CHUNK), wb), BF16),
            pltpu.VMEM((nch * HEAD_DV, wb), F32),
            pltpu.VMEM((seq, wb), F32),
        ],
        compiler_params=pltpu.CompilerParams(dimension_semantics=("parallel", "arbitrary"),
                                             vmem_limit_bytes=VMEM_LIMIT),
        name="gdn_core",
    )(proj, proj, proj, proj, ga, gat4, conv_w, conv_w, conv_w, pcol, prow,
      norm_g.reshape(1, HEAD_DV))


def _mlstm_kernel(q_ref, k_ref, v_ref, og_ref, gc_ref, gr_ref, pcol_ref, prow_ref, ng_ref,
                  o_ref, icol_ref, bcol_ref, drow_ref, c_ref, *, seq):
    nch = seq // CHUNK
    dk = HEAD_DV // 2
    grp = pl.program_id(1)
    tril, _, triu = _tri_masks(CHUNK)
    tril16 = _ones_where(tril)
    triu16 = _ones_where(triu)
    bias_row = pcol_ref[0, 0:1, :]

    def gate_cols(c, carry):
        rows = pl.ds(pl.multiple_of(c * CHUNK, CHUNK), CHUNK)
        pre = _soft_cap(gc_ref[rows, :] + bias_row)
        icol_ref[rows, :] = pre
        bcol_ref[rows, :] = _cumsum_rows(-_softplus(-pre), tril16)
        return carry

    lax.fori_loop(0, nch, gate_cols, 0, unroll=4)

    for i in range(HB):
        hh = grp * HB + i
        i_r = _soft_cap(gr_ref[hh, 0] + prow_ref[0, hh])
        f_r = _soft_cap(gr_ref[HEADS + hh, 0] + prow_ref[1, hh])
        drow_ref[i] = i_r - _cumsum_lanes(-_softplus(-f_r), triu16)

    lane = lax.broadcasted_iota(jnp.int32, (CHUNK, LANES), 1)
    ones_col = _ones_where(lane == 0)
    c_ref[...] = jnp.zeros(c_ref.shape, F32)

    chunks_per_iter = 4
    head_lanes = [(lane >= i * dk) & (lane < (i + 1) * dk) for i in range(HB)]
    cols = [slice(i * LANES, (i + 1) * LANES) for i in range(HB)]

    def step(cc, m_states):
        m_st = list(m_states)
        items = [(u, i) for u in range(chunks_per_iter) for i in range(HB)]
        chunk = [cc * chunks_per_iter + u for u in range(chunks_per_iter)]
        rows = [pl.ds(pl.multiple_of(c * CHUNK, CHUNK), CHUNK) for c in chunk]
        qp = [q_ref[r, :] for r in rows]
        kp = [k_ref[r, :] for r in rows]
        kp16 = [x.astype(BF16) for x in kp]
        bcum = [bcol_ref[rows[u], HB + i:HB + i + 1] for u, i in items]
        ig = [icol_ref[rows[u], i:i + 1] for u, i in items]
        dlog = [jnp.where(tril, b + drow_ref[i, pl.ds(chunk[u], 1), :], -jnp.inf)
                for b, (u, i) in zip(bcum, items)]
        m_intra = [jnp.max(d, axis=-1, keepdims=True) for d in dlog]
        q16 = [(jnp.where(head_lanes[i], qp[u], 0.0) * dk ** -0.5).astype(BF16) for u, i in items]
        s_raw = [_dot_nt(q, kp16[u]) for q, (u, i) in zip(q16, items)]
        g = [b[CHUNK - 1:CHUNK, :] for b in bcum]
        a = [g_ - b + i_ for g_, b, i_ in zip(g, bcum, ig)]
        a_max = [jnp.max(x, axis=0, keepdims=True) for x in a]
        m_out, w_inter, w_old, m_new = [], [], [], []
        for n, (u, i) in enumerate(items):
            m_inter = bcum[n] + m_st[i]
            m_out.append(jnp.maximum(m_inter, m_intra[n]))
            w_inter.append(jnp.exp(m_inter - m_out[n]))
            m_new.append(jnp.maximum(g[n] + m_st[i], a_max[n]))
            w_old.append(jnp.exp(g[n] + m_st[i] - m_new[n]))
            m_st[i] = m_new[n]
        vaug = [jnp.concatenate([v_ref[rows[u], cols[i]].astype(BF16), ones_col], axis=1)
                for u, i in items]
        s16 = [(s * jnp.exp(d - m)).astype(BF16) for s, d, m in zip(s_raw, dlog, m_out)]
        sv = [_dot(s, v) for s, v in zip(s16, vaug)]
        kw = [(jnp.where(head_lanes[i], kp[u], 0.0) * jnp.exp(a[n] - m_new[n])).astype(BF16)
              for n, (u, i) in enumerate(items)]
        kv = [_dot_tn(k_, v) for k_, v in zip(kw, vaug)]
        cst = [c_ref[i] for i in range(HB)]
        for n, (u, i) in enumerate(items):
            nd = w_inter[n] * _dot(q16[n], cst[i].astype(BF16)) + sv[n]
            den = nd[:, LANES:LANES + 1]
            h = nd[:, :LANES] * (1.0 / jnp.maximum(jnp.abs(den), jnp.exp(-m_out[n])))
            cst[i] = w_old[n] * cst[i] + kv[n]
            hn = _rms(h, ng_ref[:, cols[i]])
            o_ref[rows[u], cols[i]] = (_sigmoid(og_ref[rows[u], cols[i]]) * hn).astype(o_ref.dtype)
        for i in range(HB):
            c_ref[i] = cst[i]
        return tuple(m_st)

    lax.fori_loop(0, nch // chunks_per_iter, step,
                  tuple(jnp.zeros((1, 1), F32) for _ in range(HB)))


def _mlstm_core(proj, ga, gat, gate_b, norm_g, *, batch, seq):
    m = batch * seq
    nch = seq // CHUNK
    wb = HB * LANES
    g_ = GROUPS
    gat4 = gat.reshape(2 * HEADS, batch, nch, CHUNK)
    pad = jnp.zeros((GROUPS, LANES - 2 * HB), F32)
    pcol = jnp.concatenate([gate_b[0].reshape(GROUPS, HB), gate_b[1].reshape(GROUPS, HB), pad],
                           axis=1).reshape(GROUPS, 1, LANES)
    kernel = functools.partial(_mlstm_kernel, seq=seq)
    return pl.pallas_call(
        kernel, grid=(batch, g_),
        in_specs=[
            pl.BlockSpec((seq, LANES), lambda b, g: (b, g)),
            pl.BlockSpec((seq, LANES), lambda b, g: (b, g_ + g)),
            pl.BlockSpec((seq, wb), lambda b, g: (b, g_ + g)),
            pl.BlockSpec((seq, wb), lambda b, g: (b, 2 * g_ + g)),
            pl.BlockSpec((seq, LANES), lambda b, g: (b, g)),
            pl.BlockSpec((2 * HEADS, 1, nch, CHUNK), lambda b, g: (0, b, 0, 0)),
            pl.BlockSpec((1, 1, LANES), lambda b, g: (g, 0, 0)),
            pl.BlockSpec(memory_space=pltpu.SMEM),
            pl.BlockSpec((1, wb), lambda b, g: (0, g)),
        ],
        out_specs=pl.BlockSpec((seq, wb), lambda b, g: (b, g)),
        out_shape=jax.ShapeDtypeStruct((m, HEADS * HEAD_DV), BF16),
        scratch_shapes=[
            pltpu.VMEM((seq, LANES), F32),
            pltpu.VMEM((seq, LANES), F32),
            pltpu.VMEM((HB, nch, CHUNK), F32),
            pltpu.VMEM((HB, LANES, 2 * LANES), F32),
        ],
        compiler_params=pltpu.CompilerParams(dimension_semantics=("parallel", "arbitrary"),
                                             vmem_limit_bytes=VMEM_LIMIT),
        name="mlstm_core",
    )(proj, proj, proj, proj, ga, gat4, pcol, gate_b, norm_g.reshape(1, HEADS * HEAD_DV))


def _rope(x, cos, sin_lo, sin_hi):
    half = ROT_DIMS // 2
    return (x * cos + pltpu.roll(x, LANES - half, axis=1) * sin_lo
            + pltpu.roll(x, half, axis=1) * sin_hi)


def _diff_kernel(q_ref, k_ref, v_ref, tq_ref, tk_ref, lam_ref, ng_ref, o_ref,
                 kr_ref, vt_ref, q2_ref, m_ref, acc_ref, *, seq, tq, dh, lambda_init):
    qi = pl.program_id(2)
    tk = tq

    @pl.when(qi == 0)
    def _():
        def prep_tile(t, carry):
            rows = pl.ds(pl.multiple_of(t * tk, tk), tk)
            for d in range(dh):
                cs = slice(d * LANES, (d + 1) * LANES)
                kr_ref[d, rows, :] = _rope(k_ref[rows, cs], tk_ref[0, rows, :], tk_ref[1, rows, :],
                                           tk_ref[2, rows, :]).astype(BF16)
                vt_ref[d, 0:LANES, rows] = v_ref[rows, cs].T.astype(BF16)
                vt_ref[d, LANES:LANES + ONES_ROWS, rows] = _ones_where(
                    lax.broadcasted_iota(jnp.int32, (ONES_ROWS, tk), 0) == 0)
            return carry
        lax.fori_loop(0, seq // tk, prep_tile, 0)

    lane = lax.broadcasted_iota(jnp.int32, (tq, LANES), 1)
    for d in range(dh):
        cs = slice(d * LANES, (d + 1) * LANES)
        q = _rope(q_ref[:, cs], tq_ref[0], tq_ref[1], tq_ref[2]) * (DIFF_D ** -0.5 * LOG2_E)
        q2_ref[d] = jnp.concatenate(
            [jnp.where(lane < DIFF_D, q, 0.0), jnp.where(lane >= DIFF_D, q, 0.0)],
            axis=0).astype(BF16)
    m_ref[...] = jnp.full(m_ref.shape, -jnp.inf, F32)
    acc_ref[...] = jnp.zeros(acc_ref.shape, F32)

    def kv_block(j, masked):
        rows = pl.ds(pl.multiple_of(j * tk, tk), tk)
        st = [_dot_nt(kr_ref[d, rows, :], q2_ref[d]) for d in range(dh)]
        if masked:
            kpos = lax.broadcasted_iota(jnp.int32, (tk, 2 * tq), 0)
            qpos = lax.broadcasted_iota(jnp.int32, (tk, 2 * tq), 1)
            qpos = jnp.where(qpos >= tq, qpos - tq, qpos)
            shift = CHUNK.bit_length() - 1
            keep = jnp.right_shift(kpos, shift) <= jnp.right_shift(qpos, shift)
            st = [jnp.where(keep, s, -jnp.inf) for s in st]
        m_prev = [m_ref[d] for d in range(dh)]
        m_new = [jnp.maximum(m, jnp.max(s, axis=0, keepdims=True))
                 for m, s in zip(m_prev, st)]
        p16 = [jnp.exp2(s - m).astype(BF16) for s, m in zip(st, m_new)]
        pv = [_dot(vt_ref[d, :, rows], p16[d]) for d in range(dh)]
        for d in range(dh):
            acc_ref[d] = jnp.exp2(m_prev[d] - m_new[d]) * acc_ref[d] + pv[d]
            m_ref[d] = m_new[d]

    def body(j, carry):
        kv_block(j, False)
        return carry

    lax.fori_loop(0, qi, body, 0)
    kv_block(qi, True)

    lp = lam_ref[...]
    lam = (jnp.exp(jnp.sum(lp[0:1] * lp[1:2], axis=-1, keepdims=True))
           - jnp.exp(jnp.sum(lp[2:3] * lp[3:4], axis=-1, keepdims=True)) + lambda_init)
    for d in range(dh):
        acc = acc_ref[d]
        out_t = acc[:LANES] * (1.0 / acc[LANES:LANES + 1])
        o = (out_t[:, :tq] - lam * out_t[:, tq:]).T
        o_ref[:, d * LANES:(d + 1) * LANES] = (
            _rms(o, ng_ref[...]) * (1.0 - lambda_init)).astype(o_ref.dtype)


def _rope_tables(seq):
    half = ROT_DIMS // 2
    inv_freq = ROPE_THETA ** (-jnp.arange(0, ROT_DIMS, 2, dtype=F32) / ROT_DIMS)
    ang = jnp.arange(seq, dtype=F32)[:, None] * inv_freq[None, :]
    cos, sin = jnp.cos(ang), jnp.sin(ang)
    one = jnp.ones((seq, DIFF_D - ROT_DIMS), F32)
    zero = jnp.zeros((seq, DIFF_D - ROT_DIMS), F32)
    zh = jnp.zeros((seq, half), F32)
    cos_c = jnp.concatenate([cos, cos, one], axis=1)
    lo_c = jnp.concatenate([-sin, zh, zero], axis=1)
    hi_c = jnp.concatenate([zh, sin, zero], axis=1)
    rep = lambda t: jnp.concatenate([t, t], axis=1)
    return jnp.stack([rep(cos_c), rep(lo_c), rep(hi_c)], axis=0)


def _diff_core(proj, tables, lam_p, norm_g, lambda_init, *, batch, seq, tq=256, dh=4):
    m = batch * seq
    nq = seq // tq
    hg = HEADS // dh
    wb = dh * LANES
    kernel = functools.partial(_diff_kernel, seq=seq, tq=tq, dh=dh, lambda_init=lambda_init)
    return pl.pallas_call(
        kernel, grid=(batch, hg, nq),
        in_specs=[
            pl.BlockSpec((tq, wb), lambda b, h, i: (b * nq + i, h)),
            pl.BlockSpec((seq, wb), lambda b, h, i: (b, hg + h)),
            pl.BlockSpec((seq, wb), lambda b, h, i: (b, 2 * hg + h)),
            pl.BlockSpec((3, tq, LANES), lambda b, h, i: (0, i, 0)),
            pl.BlockSpec((3, seq, LANES), lambda b, h, i: (0, 0, 0)),
            pl.BlockSpec((4, DIFF_D), lambda b, h, i: (0, 0)),
            pl.BlockSpec((1, LANES), lambda b, h, i: (0, 0)),
        ],
        out_specs=pl.BlockSpec((tq, wb), lambda b, h, i: (b * nq + i, h)),
        out_shape=jax.ShapeDtypeStruct((m, HEADS * 2 * DIFF_D), BF16),
        scratch_shapes=[
            pltpu.VMEM((dh, seq, LANES), BF16),
            pltpu.VMEM((dh, LANES + ONES_ROWS, seq), BF16),
            pltpu.VMEM((dh, 2 * tq, LANES), BF16),
            pltpu.VMEM((dh, 1, 2 * tq), F32),
            pltpu.VMEM((dh, LANES + ONES_ROWS, 2 * tq), F32),
        ],
        compiler_params=pltpu.CompilerParams(
            dimension_semantics=("parallel", "parallel", "arbitrary"),
            vmem_limit_bytes=VMEM_LIMIT),
        name="diff_core",
    )(proj, proj, proj, tables, tables, lam_p, norm_g.reshape(1, 2 * DIFF_D))


def _out_proj_kernel(o_ref, w_ref, g_ref, x_ref, y_ref):
    y_ref[...] = x_ref[...] + _rms(_dot(o_ref[...], w_ref[...]), g_ref[...])


def _out_proj(o16, w16, g, x2, *, tm=1024):
    m, d = x2.shape
    return pl.pallas_call(
        _out_proj_kernel, grid=(m // tm,),
        in_specs=[pl.BlockSpec((tm, d), lambda i: (i, 0)),
                  pl.BlockSpec((d, d), lambda i: (0, 0)),
                  pl.BlockSpec((1, d), lambda i: (0, 0)),
                  pl.BlockSpec((tm, d), lambda i: (i, 0))],
        out_specs=pl.BlockSpec((tm, d), lambda i: (i, 0)),
        out_shape=jax.ShapeDtypeStruct((m, d), F32),
        compiler_params=pltpu.CompilerParams(dimension_semantics=("parallel",),
                                             vmem_limit_bytes=VMEM_LIMIT),
        name="out_proj",
    )(o16, w16, g, x2)


def _mlp_kernel(x_ref, g2_ref, w1_ref, w2_ref, g3_ref, y_ref, h_ref, acc_ref):
    j = pl.program_id(1)

    @pl.when(j == 0)
    def _():
        h_ref[...] = _rms(x_ref[...], g2_ref[...]).astype(BF16)
        acc_ref[...] = jnp.zeros(acc_ref.shape, F32)

    a = jnp.maximum(_dot(h_ref[...], w1_ref[...]), 0.0)
    acc_ref[...] += _dot((a * a).astype(BF16), w2_ref[...])

    @pl.when(j == pl.num_programs(1) - 1)
    def _():
        y_ref[...] = x_ref[...] + _rms(acc_ref[...], g3_ref[...])


def _mlp(x2, g2, w1_16, w2_16, g3, *, tm=1024, tf=1024):
    m, d = x2.shape
    f = w1_16.shape[1]
    return pl.pallas_call(
        _mlp_kernel, grid=(m // tm, f // tf),
        in_specs=[pl.BlockSpec((tm, d), lambda i, j: (i, 0)),
                  pl.BlockSpec((1, d), lambda i, j: (0, 0)),
                  pl.BlockSpec((d, tf), lambda i, j: (0, j)),
                  pl.BlockSpec((tf, d), lambda i, j: (j, 0)),
                  pl.BlockSpec((1, d), lambda i, j: (0, 0))],
        out_specs=pl.BlockSpec((tm, d), lambda i, j: (i, 0)),
        out_shape=jax.ShapeDtypeStruct((m, d), F32),
        scratch_shapes=[pltpu.VMEM((tm, d), BF16), pltpu.VMEM((tm, d), F32)],
        compiler_params=pltpu.CompilerParams(dimension_semantics=("parallel", "arbitrary"),
                                             vmem_limit_bytes=VMEM_LIMIT),
        name="mlp",
    )(x2, g2, w1_16, w2_16, g3)


def kernel(x, norm_g, mlp_w1, mlp_w2, gdn_w_in, gdn_conv, gdn_a_log, gdn_dt_bias, gdn_norm_g,
           gdn_w_out, mlstm_w_in, mlstm_gate_b, mlstm_norm_g, mlstm_w_out, diff_w_in,
           diff_lambda, diff_norm_g, diff_w_out):
    batch, seq, d = x.shape
    depth = norm_g.shape[0]
    x2 = x.reshape(batch * seq, d)
    tables = _rope_tables(seq)
    row = lambda v: v.reshape(1, -1)
    for i in range(depth):
        kind, j = i % 3, i // 3
        g_pre = row(norm_g[i, 0])
        if kind == 0:
            w_in = gdn_w_in[j]
            n_main = 4 * HEADS * HEAD_DV
            wc, wr = _gate_weights(w_in[:, n_main:n_main + HEADS], w_in[:, n_main + HEADS:])
            proj, ga, gat = _in_proj(x2, g_pre, w_in[:, :n_main].astype(BF16), wc, wr)
            o16 = _gdn_core(proj, ga, gat, gdn_conv[j], gdn_a_log[j], gdn_dt_bias[j],
                            gdn_norm_g[j], batch=batch, seq=seq)
            w_out = gdn_w_out[j]
        elif kind == 1:
            w_in = mlstm_w_in[j]
            n_main = 3 * HEADS * HEAD_DV
            wc, wr = _gate_weights(w_in[:, n_main:n_main + HEADS], w_in[:, n_main + HEADS:])
            proj, ga, gat = _in_proj(x2, g_pre, w_in[:, :n_main].astype(BF16), wc, wr)
            o16 = _mlstm_core(proj, ga, gat, mlstm_gate_b[j], mlstm_norm_g[j],
                              batch=batch, seq=seq)
            w_out = mlstm_w_out[j]
        else:
            lambda_init = 0.8 - 0.6 * math.exp(-0.3 * i)
            proj = _in_proj(x2, g_pre, diff_w_in[j].astype(BF16))
            o16 = _diff_core(proj, tables, diff_lambda[j], diff_norm_g[j], lambda_init,
                             batch=batch, seq=seq)
            w_out = diff_w_out[j]
        x2 = _out_proj(o16, w_out.astype(BF16), row(norm_g[i, 1]), x2)
        x2 = _mlp(x2, row(norm_g[i, 2]), mlp_w1[i].astype(BF16), mlp_w2[i].astype(BF16),
                  row(norm_g[i, 3]))
    return x2.reshape(batch, seq, d)
```

```python
import functools
import math

import jax
import jax.numpy as jnp
from jax import lax
from jax.experimental import pallas as pl
from jax.experimental.pallas import tpu as pltpu

F32 = jnp.float32
BF16 = jnp.bfloat16

D_MODEL = 1024
D_FF = 4 * D_MODEL
HEADS = 8
HEAD_DV = 128
CHUNK = 64
NORM_EPS = 1e-6
GATE_CAP = 15.0
GDN_CONV = 4
ROPE_THETA = 500000.0
DIFF_D = 64
ROT_DIMS = DIFF_D // 4
LANES = 128
ONES_ROWS = 16
LOG2_E = math.log2(math.e)
HB = 2
GROUPS = HEADS // HB
VMEM_LIMIT = 56 * 1024 * 1024

_NT = (((1,), (1,)), ((), ()))
_TN = (((0,), (0,)), ((), ()))


def _dot(a, b):
    return jnp.dot(a, b, preferred_element_type=F32)


def _dot_nt(a, b):
    return lax.dot_general(a, b, _NT, preferred_element_type=F32)


def _dot_tn(a, b):
    return lax.dot_general(a, b, _TN, preferred_element_type=F32)


def _rms(x, g):
    ms = jnp.mean(x * x, axis=-1, keepdims=True)
    return x * lax.rsqrt(ms + NORM_EPS) * g


def _sigmoid(x):
    return 1.0 / (1.0 + jnp.exp(-x))


def _softplus(x):
    return jnp.maximum(x, 0.0) + jnp.log1p(jnp.exp(-jnp.abs(x)))


def _soft_cap(x):
    return GATE_CAP * jnp.tanh(x / GATE_CAP)


def _split3(x):
    x1 = x.astype(BF16)
    r1 = x - x1.astype(F32)
    x2 = r1.astype(BF16)
    r2 = r1 - x2.astype(F32)
    return x1, x2, r2.astype(BF16)


def _cumsum_rows(x, tril16):
    a, b, c = _split3(x)
    return _dot(tril16, a) + _dot(tril16, b) + _dot(tril16, c)


def _cumsum_lanes(x, triu16):
    a, b, c = _split3(x)
    return _dot(a, triu16) + _dot(b, triu16) + _dot(c, triu16)


def _split2(x):
    hi = x.astype(BF16)
    return hi, (x - hi.astype(F32)).astype(BF16)


def _dot3(a, b):
    ah, al = _split2(a)
    bh, bl = _split2(b)
    return _dot(ah, bh) + _dot(ah, bl) + _dot(al, bh)


def _dot2(a, b):
    ah = a.astype(BF16)
    bh, bl = _split2(b)
    return _dot(ah, bh) + _dot(ah, bl)


def _unit_lower_inverse_minus_identity(a, between_levels=None):
    n = a[0].shape[0]
    r = lax.broadcasted_iota(jnp.int32, (n, n), 0)
    c = lax.broadcasted_iota(jnp.int32, (n, n), 1)
    pair = jnp.right_shift(r, 1) == jnp.right_shift(c, 1)
    a = [x.astype(BF16).astype(F32) for x in a]
    inv_m1 = [-jnp.where(pair, x, 0.0) for x in a]
    s = 2
    while s < n:
        sh = s.bit_length() - 1
        merge = ((jnp.right_shift(r, sh + 1) == jnp.right_shift(c, sh + 1))
                 & (jnp.right_shift(r, sh) != jnp.right_shift(c, sh)))
        off = [jnp.where(merge, x, 0.0) for x in a]
        t = [_dot2(o, m) for o, m in zip(off, inv_m1)]
        y = [o + x for o, x in zip(off, t)]
        t = [_dot3(m, x) for m, x in zip(inv_m1, y)]
        inv_m1 = [m - (x + z) for m, x, z in zip(inv_m1, y, t)]
        if between_levels is not None:
            between_levels()
        s *= 2
    return inv_m1


def _tri_masks(n):
    r = lax.broadcasted_iota(jnp.int32, (n, n), 0)
    c = lax.broadcasted_iota(jnp.int32, (n, n), 1)
    return r >= c, r > c, r <= c


def _ones_where(mask):
    return jnp.where(mask, 1.0, 0.0).astype(BF16)


def _in_proj_kernel(x_ref, g_ref, w_ref, o_ref, h_ref):
    @pl.when(pl.program_id(1) == 0)
    def _():
        h_ref[...] = _rms(x_ref[...], g_ref[...]).astype(BF16)

    o_ref[...] = _dot(h_ref[...], w_ref[...]).astype(o_ref.dtype)


def _in_proj_gates_kernel(x_ref, g_ref, w_ref, wc_ref, wr_ref, o_ref, oc_ref, or_ref, h_ref):
    @pl.when(pl.program_id(1) == 0)
    def _():
        h = _rms(x_ref[...], g_ref[...]).astype(BF16)
        h_ref[...] = h
        oc_ref[...] = _dot(h, wc_ref[...])
        or_ref[...] = _dot_nt(wr_ref[...], h)

    o_ref[...] = _dot(h_ref[...], w_ref[...]).astype(o_ref.dtype)


def _in_proj(x2, g, w16, wc16=None, wr16=None, *, tm=512):
    m, d = x2.shape
    n = w16.shape[1]
    tn = n
    grid = (m // tm, n // tn)
    x_spec = pl.BlockSpec((tm, d), lambda i, j: (i, 0))
    g_spec = pl.BlockSpec((1, d), lambda i, j: (0, 0))
    w_spec = pl.BlockSpec((d, tn), lambda i, j: (0, j))
    o_spec = pl.BlockSpec((tm, tn), lambda i, j: (i, j))
    params = pltpu.CompilerParams(dimension_semantics=("parallel", "arbitrary"),
                                  vmem_limit_bytes=VMEM_LIMIT)
    if wc16 is None:
        return pl.pallas_call(
            _in_proj_kernel, grid=grid,
            in_specs=[x_spec, g_spec, w_spec], out_specs=o_spec,
            out_shape=jax.ShapeDtypeStruct((m, n), F32),
            scratch_shapes=[pltpu.VMEM((tm, d), BF16)],
            compiler_params=params, name="in_proj",
        )(x2, g, w16)
    nc = wc16.shape[1]
    nr = wr16.shape[0]
    return pl.pallas_call(
        _in_proj_gates_kernel, grid=grid,
        in_specs=[x_spec, g_spec, w_spec,
                  pl.BlockSpec((d, nc), lambda i, j: (0, 0)),
                  pl.BlockSpec((nr, d), lambda i, j: (0, 0))],
        out_specs=[o_spec,
                   pl.BlockSpec((tm, nc), lambda i, j: (i, 0)),
                   pl.BlockSpec((nr, tm), lambda i, j: (0, i))],
        out_shape=[jax.ShapeDtypeStruct((m, n), F32),
                   jax.ShapeDtypeStruct((m, nc), F32),
                   jax.ShapeDtypeStruct((nr, m), F32)],
        scratch_shapes=[pltpu.VMEM((tm, d), BF16)],
        compiler_params=params, name="in_proj_gates",
    )(x2, g, w16, wc16, wr16)


def _gate_weights(w_a, w_b):
    d = w_a.shape[0]
    a = w_a.reshape(d, GROUPS, HB)
    b = w_b.reshape(d, GROUPS, HB)
    pad = jnp.zeros((d, GROUPS, LANES - 2 * HB), w_a.dtype)
    wc = jnp.concatenate([a, b, pad], axis=2).reshape(d, GROUPS * LANES)
    wr = jnp.concatenate([w_a, w_b], axis=1).T
    return wc.astype(BF16), wr.astype(BF16)


def _group_rows(p_a, p_b):
    pad = jnp.zeros((GROUPS, LANES - HB), F32)
    ra = jnp.concatenate([p_a.reshape(GROUPS, HB), pad], axis=1)
    rb = jnp.concatenate([p_b.reshape(GROUPS, HB), pad], axis=1)
    return jnp.stack([ra, rb], axis=1)


def _gdn_kernel(q_ref, k_ref, v_ref, z_ref, gc_ref, gr_ref, cwq_ref, cwk_ref, cwv_ref,
                pcol_ref, prow_ref, ng_ref, o_ref,
                xp_ref, qn_ref, kn_ref, vn_ref, bcol_ref, gcol_ref, grow_ref,
                fold_ref, sadd_ref, oadd_ref, *, seq):
    nch = seq // CHUNK
    fold_rows = HEAD_DV + CHUNK
    grp = pl.program_id(1)
    tril, strict, triu = _tri_masks(CHUNK)
    tril16 = _ones_where(tril)
    triu16 = _ones_where(triu)

    alog_row = pcol_ref[0, 0:1, :]
    dtb_row = pcol_ref[0, 1:2, :]
    neg_decay_rate = -jnp.exp(alog_row)

    def gate_cols(c, carry):
        rows = pl.ds(pl.multiple_of(c * CHUNK, CHUNK), CHUNK)
        ga = gc_ref[rows, :]
        g = neg_decay_rate * _softplus(ga + dtb_row)
        gcol_ref[rows, :] = _cumsum_rows(g, tril16)
        bcol_ref[rows, :] = _sigmoid(ga)
        return carry

    lax.fori_loop(0, nch, gate_cols, 0, unroll=4)

    for i in range(HB):
        hh = grp * HB + i
        a_r = gr_ref[hh, 0]
        alog = jnp.full((nch, CHUNK), prow_ref[0, hh], F32)
        g_r = -jnp.exp(alog) * _softplus(a_r + prow_ref[1, hh])
        grow_ref[i] = _cumsum_lanes(g_r, triu16)

    tile = 256
    xp_ref[0:8, :] = jnp.zeros((8, LANES), F32)
    jobs = []
    for i in range(HB):
        cs = slice(i * LANES, (i + 1) * LANES)
        jobs.append((q_ref, cwq_ref, qn_ref, cs, HEAD_DV ** -0.5))
        jobs.append((k_ref, cwk_ref, kn_ref, cs, 1.0))
        jobs.append((v_ref, cwv_ref, vn_ref, cs, None))
    for src, cw_ref, dst, cs, scale in jobs:
        xp_ref[8:seq + 8, :] = src[:, cs]
        cw = cw_ref[:, cs]

        def conv_tile(t, carry, dst=dst, cs=cs, cw=cw, scale=scale):
            r0 = pl.multiple_of(t * tile, tile)
            y = cw[3:4, :] * xp_ref[pl.ds(r0 + 8, tile), :]
            for j in range(GDN_CONV - 1):
                y = y + cw[j:j + 1, :] * xp_ref[pl.ds(r0 + 5 + j, tile), :]
            half_y = 0.5 * y
            y = half_y + half_y * jnp.tanh(half_y)
            if scale is not None:
                y = y * lax.rsqrt(jnp.sum(y * y, axis=-1, keepdims=True) + NORM_EPS)
                if scale != 1.0:
                    y = y * scale
            dst[pl.ds(r0, tile), cs] = y
            return carry

        lax.fori_loop(0, seq // tile, conv_tile, 0, unroll=2)

    chunks_per_iter = 8

    def intra(cc, side_work=()):
        pending = list(side_work)

        def emit_side():
            if pending:
                pending.pop(0)()

        items = []
        for u in range(chunks_per_iter):
            c = cc * chunks_per_iter + u
            rows = pl.ds(pl.multiple_of(c * CHUNK, CHUNK), CHUNK)
            for i in range(HB):
                items.append((c, rows, i, slice(i * LANES, (i + 1) * LANES)))
        q = [qn_ref[rows, cs] for _, rows, _, cs in items]
        k = [kn_ref[rows, cs] for _, rows, _, cs in items]
        v = [vn_ref[rows, cs] for _, rows, _, cs in items]
        beta = [bcol_ref[rows, HB + i:HB + i + 1] for _, rows, i, _ in items]
        gc = [gcol_ref[rows, i:i + 1] for _, rows, i, _ in items]
        gr = [grow_ref[i, pl.ds(c, 1), :] for c, _, i, _ in items]
        decay = [jnp.where(tril, jnp.exp(jnp.where(tril, a - b, 0.0)), 0.0) for a, b in zip(gc, gr)]
        kb = [a * b for a, b in zip(k, beta)]
        k16 = [a.astype(BF16) for a in k]
        g_kk = [_dot_nt(a.astype(BF16), b) for a, b in zip(kb, k16)]
        g_qk = [_dot_nt(a.astype(BF16), b) for a, b in zip(q, k16)]
        emit_side()
        a_mat = [jnp.where(strict, a * d, 0.0) for a, d in zip(g_kk, decay)]
        inv_m1 = _unit_lower_inverse_minus_identity(a_mat, emit_side)
        egc = [jnp.exp(a) for a in gc]
        rhs = [jnp.concatenate([a * b, c_ * e], axis=1)
               for a, b, c_, e in zip(v, beta, kb, egc)]
        sol = [a + _dot3(m, a) for m, a in zip(inv_m1, rhs)]
        emit_side()
        sol16 = [a.astype(BF16) for a in sol]
        kst16 = [(a * jnp.exp(g_[CHUNK - 1:CHUNK, :] - g_)).astype(BF16) for a, g_ in zip(k, gc)]
        att16 = [jnp.where(tril, a * d, 0.0).astype(BF16) for a, d in zip(g_qk, decay)]
        kst_sol = [_dot_tn(a, b) for a, b in zip(kst16, sol16)]
        att_sol = [_dot(a, b) for a, b in zip(att16, sol16)]
        while pending:
            emit_side()
        for n, (c, rows, i, cs) in enumerate(items):
            t_rows = pl.ds(pl.multiple_of(c * fold_rows, CHUNK), HEAD_DV)
            q_rows = pl.ds(pl.multiple_of(c * fold_rows + HEAD_DV, CHUNK), CHUNK)
            fold_ref[t_rows, cs] = (-kst_sol[n][:, LANES:]).astype(BF16)
            fold_ref[q_rows, cs] = (q[n] * egc[n] - att_sol[n][:, LANES:]).astype(BF16)
            sadd_ref[pl.ds(pl.multiple_of(c * HEAD_DV, HEAD_DV), HEAD_DV), cs] = kst_sol[n][:, :LANES]
            oadd_ref[rows, cs] = att_sol[n][:, :LANES]

    ng = ng_ref[...]

    def step(c, states):
        r0 = pl.multiple_of(c * CHUNK, CHUNK)
        rows = pl.ds(r0, CHUNK)
        heads = range(HB)
        cols = [slice(i * LANES, (i + 1) * LANES) for i in heads]
        f_rows = pl.ds(pl.multiple_of(c * fold_rows, CHUNK), fold_rows)
        s_rows = pl.ds(pl.multiple_of(c * HEAD_DV, HEAD_DV), HEAD_DV)
        r = [_dot(fold_ref[f_rows, cols[i]], states[i].astype(BF16)) for i in heads]
        new_states = []
        for i in heads:
            egl = jnp.exp(gcol_ref[pl.ds(r0 + CHUNK - 1, 1), i:i + 1])
            new_states.append(states[i] * egl + r[i][:HEAD_DV] + sadd_ref[s_rows, cols[i]])
            z = z_ref[rows, cols[i]]
            o = r[i][HEAD_DV:] + oadd_ref[rows, cols[i]]
            o_ref[rows, cols[i]] = (_rms(o, ng) * (z * _sigmoid(z))).astype(o_ref.dtype)
        return tuple(new_states)

    n_groups = nch // chunks_per_iter

    def fused(cc, states):
        box = [states]

        def scan_piece(k):
            def run():
                box[0] = step((cc - 1) * chunks_per_iter + k, box[0])
            return run

        intra(cc, [scan_piece(k) for k in range(chunks_per_iter)])
        return box[0]

    init = tuple(jnp.zeros((HEAD_DV, HEAD_DV), F32) for _ in range(HB))
    intra(jnp.int32(0))
    states = lax.fori_loop(1, n_groups, fused, init)
    lax.fori_loop((n_groups - 1) * chunks_per_iter, nch, step, states)


def _gdn_core(proj, ga, gat, conv_w, a_log, dt_bias, norm_g, *, batch, seq):
    m = batch * seq
    nch = seq // CHUNK
    wb = HB * LANES
    g_ = GROUPS
    gat4 = gat.reshape(2 * HEADS, batch, nch, CHUNK)
    pcol = _group_rows(a_log, dt_bias)
    prow = jnp.stack([a_log, dt_bias], axis=0)
    kernel = functools.partial(_gdn_kernel, seq=seq)
    return pl.pallas_call(
        kernel, grid=(batch, g_),
        in_specs=[
            pl.BlockSpec((seq, wb), lambda b, g: (b, g)),
            pl.BlockSpec((seq, wb), lambda b, g: (b, g_ + g)),
            pl.BlockSpec((seq, wb), lambda b, g: (b, 2 * g_ + g)),
            pl.BlockSpec((seq, wb), lambda b, g: (b, 3 * g_ + g)),
            pl.BlockSpec((seq, LANES), lambda b, g: (b, g)),
            pl.BlockSpec((2 * HEADS, 1, nch, CHUNK), lambda b, g: (0, b, 0, 0)),
            pl.BlockSpec((GDN_CONV, wb), lambda b, g: (0, g)),
            pl.BlockSpec((GDN_CONV, wb), lambda b, g: (0, g_ + g)),
            pl.BlockSpec((GDN_CONV, wb), lambda b, g: (0, 2 * g_ + g)),
            pl.BlockSpec((1, 2, LANES), lambda b, g: (g, 0, 0)),
            pl.BlockSpec(memory_space=pltpu.SMEM),
            pl.BlockSpec((1, LANES), lambda b, g: (0, 0)),
        ],
        out_specs=pl.BlockSpec((seq, wb), lambda b, g: (b, g)),
        out_shape=jax.ShapeDtypeStruct((m, HEADS * HEAD_DV), BF16),
        scratch_shapes=[
            pltpu.VMEM((seq + 8, LANES), F32),
            pltpu.VMEM((seq, wb), F32),
            pltpu.VMEM((seq, wb), F32),
            pltpu.VMEM((seq, wb), F32),
            pltpu.VMEM((seq, LANES), F32),
            pltpu.VMEM((seq, LANES), F32),
            pltpu.VMEM((HB, nch, CHUNK), F32),
            pltpu.VMEM((nch * (HEAD_DV + CHUNK), wb), BF16),
            pltpu.VMEM((nch * HEAD_DV, wb), F32),
            pltpu.VMEM((seq, wb), F32),
        ],
        compiler_params=pltpu.CompilerParams(dimension_semantics=("parallel", "arbitrary"),
                                             vmem_limit_bytes=VMEM_LIMIT),
        name="gdn_core",
    )(proj, proj, proj, proj, ga, gat4, conv_w, conv_w, conv_w, pcol, prow,
      norm_g.reshape(1, HEAD_DV))


def _mlstm_kernel(q_ref, k_ref, v_ref, og_ref, gc_ref, gr_ref, pcol_ref, prow_ref, ng_ref,
                  o_ref, icol_ref, bcol_ref, drow_ref, c_ref, *, seq):
    nch = seq // CHUNK
    dk = HEAD_DV // 2
    grp = pl.program_id(1)
    tril, _, triu = _tri_masks(CHUNK)
    tril16 = _ones_where(tril)
    triu16 = _ones_where(triu)
    bias_row = pcol_ref[0, 0:1, :]

    def gate_cols(c, carry):
        rows = pl.ds(pl.multiple_of(c * CHUNK, CHUNK), CHUNK)
        pre = _soft_cap(gc_ref[rows, :] + bias_row)
        icol_ref[rows, :] = pre
        bcol_ref[rows, :] = _cumsum_rows(-_softplus(-pre), tril16)
        return carry

    lax.fori_loop(0, nch, gate_cols, 0, unroll=4)

    for i in range(HB):
        hh = grp * HB + i
        i_r = _soft_cap(gr_ref[hh, 0] + prow_ref[0, hh])
        f_r = _soft_cap(gr_ref[HEADS + hh, 0] + prow_ref[1, hh])
        drow_ref[i] = i_r - _cumsum_lanes(-_softplus(-f_r), triu16)

    lane = lax.broadcasted_iota(jnp.int32, (CHUNK, LANES), 1)
    ones_col = _ones_where(lane == 0)
    c_ref[...] = jnp.zeros(c_ref.shape, F32)

    chunks_per_iter = 4
    head_lanes = [(lane >= i * dk) & (lane < (i + 1) * dk) for i in range(HB)]
    cols = [slice(i * LANES, (i + 1) * LANES) for i in range(HB)]

    def step(cc, m_states):
        m_st = list(m_states)
        items = [(u, i) for u in range(chunks_per_iter) for i in range(HB)]
        chunk = [cc * chunks_per_iter + u for u in range(chunks_per_iter)]
        rows = [pl.ds(pl.multiple_of(c * CHUNK, CHUNK), CHUNK) for c in chunk]
        qp = [q_ref[r, :] for r in rows]
        kp = [k_ref[r, :] for r in rows]
        kp16 = [x.astype(BF16) for x in kp]
        bcum = [bcol_ref[rows[u], HB + i:HB + i + 1] for u, i in items]
        ig = [icol_ref[rows[u], i:i + 1] for u, i in items]
        dlog = [jnp.where(tril, b + drow_ref[i, pl.ds(chunk[u], 1), :], -jnp.inf)
                for b, (u, i) in zip(bcum, items)]
        m_intra = [jnp.max(d, axis=-1, keepdims=True) for d in dlog]
        q16 = [(jnp.where(head_lanes[i], qp[u], 0.0) * dk ** -0.5).astype(BF16) for u, i in items]
        s_raw = [_dot_nt(q, kp16[u]) for q, (u, i) in zip(q16, items)]
        g = [b[CHUNK - 1:CHUNK, :] for b in bcum]
        a = [g_ - b + i_ for g_, b, i_ in zip(g, bcum, ig)]
        a_max = [jnp.max(x, axis=0, keepdims=True) for x in a]
        m_out, w_inter, w_old, m_new = [], [], [], []
        for n, (u, i) in enumerate(items):
            m_inter = bcum[n] + m_st[i]
            m_out.append(jnp.maximum(m_inter, m_intra[n]))
            w_inter.append(jnp.exp(m_inter - m_out[n]))
            m_new.append(jnp.maximum(g[n] + m_st[i], a_max[n]))
            w_old.append(jnp.exp(g[n] + m_st[i] - m_new[n]))
            m_st[i] = m_new[n]
        vaug = [jnp.concatenate([v_ref[rows[u], cols[i]].astype(BF16), ones_col], axis=1)
                for u, i in items]
        s16 = [(s * jnp.exp(d - m)).astype(BF16) for s, d, m in zip(s_raw, dlog, m_out)]
        sv = [_dot(s, v) for s, v in zip(s16, vaug)]
        kw = [(jnp.where(head_lanes[i], kp[u], 0.0) * jnp.exp(a[n] - m_new[n])).astype(BF16)
              for n, (u, i) in enumerate(items)]
        kv = [_dot_tn(k_, v) for k_, v in zip(kw, vaug)]
        cst = [c_ref[i] for i in range(HB)]
        for n, (u, i) in enumerate(items):
            nd = w_inter[n] * _dot(q16[n], cst[i].astype(BF16)) + sv[n]
            den = nd[:, LANES:LANES + 1]
            h = nd[:, :LANES] * (1.0 / jnp.maximum(jnp.abs(den), jnp.exp(-m_out[n])))
            cst[i] = w_old[n] * cst[i] + kv[n]
            hn = _rms(h, ng_ref[:, cols[i]])
            o_ref[rows[u], cols[i]] = (_sigmoid(og_ref[rows[u], cols[i]]) * hn).astype(o_ref.dtype)
        for i in range(HB):
            c_ref[i] = cst[i]
        return tuple(m_st)

    lax.fori_loop(0, nch // chunks_per_iter, step,
                  tuple(jnp.zeros((1, 1), F32) for _ in range(HB)))


def _mlstm_core(proj, ga, gat, gate_b, norm_g, *, batch, seq):
    m = batch * seq
    nch = seq // CHUNK
    wb = HB * LANES
    g_ = GROUPS
    gat4 = gat.reshape(2 * HEADS, batch, nch, CHUNK)
    pad = jnp.zeros((GROUPS, LANES - 2 * HB), F32)
    pcol = jnp.concatenate([gate_b[0].reshape(GROUPS, HB), gate_b[1].reshape(GROUPS, HB), pad],
                           axis=1).reshape(GROUPS, 1, LANES)
    kernel = functools.partial(_mlstm_kernel, seq=seq)
    return pl.pallas_call(
        kernel, grid=(batch, g_),
        in_specs=[
            pl.BlockSpec((seq, LANES), lambda b, g: (b, g)),
            pl.BlockSpec((seq, LANES), lambda b, g: (b, g_ + g)),
            pl.BlockSpec((seq, wb), lambda b, g: (b, g_ + g)),
            pl.BlockSpec((seq, wb), lambda b, g: (b, 2 * g_ + g)),
            pl.BlockSpec((seq, LANES), lambda b, g: (b, g)),
            pl.BlockSpec((2 * HEADS, 1, nch, CHUNK), lambda b, g: (0, b, 0, 0)),
            pl.BlockSpec((1, 1, LANES), lambda b, g: (g, 0, 0)),
            pl.BlockSpec(memory_space=pltpu.SMEM),
            pl.BlockSpec((1, wb), lambda b, g: (0, g)),
        ],
        out_specs=pl.BlockSpec((seq, wb), lambda b, g: (b, g)),
        out_shape=jax.ShapeDtypeStruct((m, HEADS * HEAD_DV), BF16),
        scratch_shapes=[
            pltpu.VMEM((seq, LANES), F32),
            pltpu.VMEM((seq, LANES), F32),
            pltpu.VMEM((HB, nch, CHUNK), F32),
            pltpu.VMEM((HB, LANES, 2 * LANES), F32),
        ],
        compiler_params=pltpu.CompilerParams(dimension_semantics=("parallel", "arbitrary"),
                                             vmem_limit_bytes=VMEM_LIMIT),
        name="mlstm_core",
    )(proj, proj, proj, proj, ga, gat4, pcol, gate_b, norm_g.reshape(1, HEADS * HEAD_DV))


def _rope(x, cos, sin_lo, sin_hi):
    half = ROT_DIMS // 2
    return (x * cos + pltpu.roll(x, LANES - half, axis=1) * sin_lo
            + pltpu.roll(x, half, axis=1) * sin_hi)


def _diff_kernel(q_ref, k_ref, v_ref, tq_ref, tk_ref, lam_ref, ng_ref, o_ref,
                 kr_ref, vt_ref, q2_ref, m_ref, acc_ref, *, seq, tq, dh, lambda_init):
    qi = pl.program_id(2)
    tk = tq

    @pl.when(qi == 0)
    def _():
        def prep_tile(t, carry):
            rows = pl.ds(pl.multiple_of(t * tk, tk), tk)
            for d in range(dh):
                cs = slice(d * LANES, (d + 1) * LANES)
                kr_ref[d, rows, :] = _rope(k_ref[rows, cs], tk_ref[0, rows, :], tk_ref[1, rows, :],
                                           tk_ref[2, rows, :]).astype(BF16)
                vt_ref[d, 0:LANES, rows] = v_ref[rows, cs].T.astype(BF16)
                vt_ref[d, LANES:LANES + ONES_ROWS, rows] = _ones_where(
                    lax.broadcasted_iota(jnp.int32, (ONES_ROWS, tk), 0) == 0)
            return carry
        lax.fori_loop(0, seq // tk, prep_tile, 0)

    lane = lax.broadcasted_iota(jnp.int32, (tq, LANES), 1)
    for d in range(dh):
        cs = slice(d * LANES, (d + 1) * LANES)
        q = _rope(q_ref[:, cs], tq_ref[0], tq_ref[1], tq_ref[2]) * (DIFF_D ** -0.5 * LOG2_E)
        q2_ref[d] = jnp.concatenate(
            [jnp.where(lane < DIFF_D, q, 0.0), jnp.where(lane >= DIFF_D, q, 0.0)],
            axis=0).astype(BF16)
    m_ref[...] = jnp.full(m_ref.shape, -jnp.inf, F32)
    acc_ref[...] = jnp.zeros(acc_ref.shape, F32)

    def kv_block(j, masked):
        rows = pl.ds(pl.multiple_of(j * tk, tk), tk)
        st = [_dot_nt(kr_ref[d, rows, :], q2_ref[d]) for d in range(dh)]
        if masked:
            kpos = lax.broadcasted_iota(jnp.int32, (tk, 2 * tq), 0)
            qpos = lax.broadcasted_iota(jnp.int32, (tk, 2 * tq), 1)
            qpos = jnp.where(qpos >= tq, qpos - tq, qpos)
            shift = CHUNK.bit_length() - 1
            keep = jnp.right_shift(kpos, shift) <= jnp.right_shift(qpos, shift)
            st = [jnp.where(keep, s, -jnp.inf) for s in st]
        m_prev = [m_ref[d] for d in range(dh)]
        m_new = [jnp.maximum(m, jnp.max(s, axis=0, keepdims=True))
                 for m, s in zip(m_prev, st)]
        p16 = [jnp.exp2(s - m).astype(BF16) for s, m in zip(st, m_new)]
        pv = [_dot(vt_ref[d, :, rows], p16[d]) for d in range(dh)]
        for d in range(dh):
            acc_ref[d] = jnp.exp2(m_prev[d] - m_new[d]) * acc_ref[d] + pv[d]
            m_ref[d] = m_new[d]

    def body(j, carry):
        kv_block(j, False)
        return carry

    lax.fori_loop(0, qi, body, 0)
    kv_block(qi, True)

    lp = lam_ref[...]
    lam = (jnp.exp(jnp.sum(lp[0:1] * lp[1:2], axis=-1, keepdims=True))
           - jnp.exp(jnp.sum(lp[2:3] * lp[3:4], axis=-1, keepdims=True)) + lambda_init)
    for d in range(dh):
        acc = acc_ref[d]
        out_t = acc[:LANES] * (1.0 / acc[LANES:LANES + 1])
        o = (out_t[:, :tq] - lam * out_t[:, tq:]).T
        o_ref[:, d * LANES:(d + 1) * LANES] = (
            _rms(o, ng_ref[...]) * (1.0 - lambda_init)).astype(o_ref.dtype)


def _rope_tables(seq):
    half = ROT_DIMS // 2
    inv_freq = ROPE_THETA ** (-jnp.arange(0, ROT_DIMS, 2, dtype=F32) / ROT_DIMS)
    ang = jnp.arange(seq, dtype=F32)[:, None] * inv_freq[None, :]
    cos, sin = jnp.cos(ang), jnp.sin(ang)
    one = jnp.ones((seq, DIFF_D - ROT_DIMS), F32)
    zero = jnp.zeros((seq, DIFF_D - ROT_DIMS), F32)
    zh = jnp.zeros((seq, half), F32)
    cos_c = jnp.concatenate([cos, cos, one], axis=1)
    lo_c = jnp.concatenate([-sin, zh, zero], axis=1)
    hi_c = jnp.concatenate([zh, sin, zero], axis=1)
    rep = lambda t: jnp.concatenate([t, t], axis=1)
    return jnp.stack([rep(cos_c), rep(lo_c), rep(hi_c)], axis=0)


def _diff_core(proj, tables, lam_p, norm_g, lambda_init, *, batch, seq, tq=256, dh=4):
    m = batch * seq
    nq = seq // tq
    hg = HEADS // dh
    wb = dh * LANES
    kernel = functools.partial(_diff_kernel, seq=seq, tq=tq, dh=dh, lambda_init=lambda_init)
    return pl.pallas_call(
        kernel, grid=(batch, hg, nq),
        in_specs=[
            pl.BlockSpec((tq, wb), lambda b, h, i: (b * nq + i, h)),
            pl.BlockSpec((seq, wb), lambda b, h, i: (b, hg + h)),
            pl.BlockSpec((seq, wb), lambda b, h, i: (b, 2 * hg + h)),
            pl.BlockSpec((3, tq, LANES), lambda b, h, i: (0, i, 0)),
            pl.BlockSpec((3, seq, LANES), lambda b, h, i: (0, 0, 0)),
            pl.BlockSpec((4, DIFF_D), lambda b, h, i: (0, 0)),
            pl.BlockSpec((1, LANES), lambda b, h, i: (0, 0)),
        ],
        out_specs=pl.BlockSpec((tq, wb), lambda b, h, i: (b * nq + i, h)),
        out_shape=jax.ShapeDtypeStruct((m, HEADS * 2 * DIFF_D), BF16),
        scratch_shapes=[
            pltpu.VMEM((dh, seq, LANES), BF16),
            pltpu.VMEM((dh, LANES + ONES_ROWS, seq), BF16),
            pltpu.VMEM((dh, 2 * tq, LANES), BF16),
            pltpu.VMEM((dh, 1, 2 * tq), F32),
            pltpu.VMEM((dh, LANES + ONES_ROWS, 2 * tq), F32),
        ],
        compiler_params=pltpu.CompilerParams(
            dimension_semantics=("parallel", "parallel", "arbitrary"),
            vmem_limit_bytes=VMEM_LIMIT),
        name="diff_core",
    )(proj, proj, proj, tables, tables, lam_p, norm_g.reshape(1, 2 * DIFF_D))


def _out_proj_kernel(o_ref, w_ref, g_ref, x_ref, y_ref):
    y_ref[...] = x_ref[...] + _rms(_dot(o_ref[...], w_ref[...]), g_ref[...])


def _out_proj(o16, w16, g, x2, *, tm=1024):
    m, d = x2.shape
    return pl.pallas_call(
        _out_proj_kernel, grid=(m // tm,),
        in_specs=[pl.BlockSpec((tm, d), lambda i: (i, 0)),
                  pl.BlockSpec((d, d), lambda i: (0, 0)),
                  pl.BlockSpec((1, d), lambda i: (0, 0)),
                  pl.BlockSpec((tm, d), lambda i: (i, 0))],
        out_specs=pl.BlockSpec((tm, d), lambda i: (i, 0)),
        out_shape=jax.ShapeDtypeStruct((m, d), F32),
        compiler_params=pltpu.CompilerParams(dimension_semantics=("parallel",),
                                             vmem_limit_bytes=VMEM_LIMIT),
        name="out_proj",
    )(o16, w16, g, x2)


def _mlp_kernel(x_ref, o_ref, wo_ref, g1_ref, g2_ref, w1_ref, w2_ref, g3_ref, y_ref,
                h_ref, acc_ref, x1_ref):
    j = pl.program_id(1)

    @pl.when(j == 0)
    def _():
        x1 = x_ref[...] + _rms(_dot(o_ref[...], wo_ref[...]), g1_ref[...])
        x1_ref[...] = x1
        h_ref[...] = _rms(x1, g2_ref[...]).astype(BF16)
        acc_ref[...] = jnp.zeros(acc_ref.shape, F32)

    a = jnp.maximum(_dot(h_ref[...], w1_ref[...]), 0.0)
    acc_ref[...] += _dot((a * a).astype(BF16), w2_ref[...])

    @pl.when(j == pl.num_programs(1) - 1)
    def _():
        y_ref[...] = x1_ref[...] + _rms(acc_ref[...], g3_ref[...])


def _out_proj_mlp(x2, o16, wo16, g1, g2, w1_16, w2_16, g3, *, tm=1024, tf=1024):
    m, d = x2.shape
    f = w1_16.shape[1]
    row_tile = pl.BlockSpec((tm, d), lambda i, j: (i, 0))
    gain = pl.BlockSpec((1, d), lambda i, j: (0, 0))
    return pl.pallas_call(
        _mlp_kernel, grid=(m // tm, f // tf),
        in_specs=[row_tile, row_tile,
                  pl.BlockSpec((d, d), lambda i, j: (0, 0)),
                  gain, gain,
                  pl.BlockSpec((d, tf), lambda i, j: (0, j)),
                  pl.BlockSpec((tf, d), lambda i, j: (j, 0)),
                  gain],
        out_specs=row_tile,
        out_shape=jax.ShapeDtypeStruct((m, d), F32),
        scratch_shapes=[pltpu.VMEM((tm, d), BF16), pltpu.VMEM((tm, d), F32),
                        pltpu.VMEM((tm, d), F32)],
        compiler_params=pltpu.CompilerParams(dimension_semantics=("parallel", "arbitrary"),
                                             vmem_limit_bytes=VMEM_LIMIT),
        name="out_proj_mlp",
    )(x2, o16, wo16, g1, g2, w1_16, w2_16, g3)


def kernel(x, norm_g, mlp_w1, mlp_w2, gdn_w_in, gdn_conv, gdn_a_log, gdn_dt_bias, gdn_norm_g,
           gdn_w_out, mlstm_w_in, mlstm_gate_b, mlstm_norm_g, mlstm_w_out, diff_w_in,
           diff_lambda, diff_norm_g, diff_w_out):
    batch, seq, d = x.shape
    depth = norm_g.shape[0]
    x2 = x.reshape(batch * seq, d)
    tables = _rope_tables(seq)
    row = lambda v: v.reshape(1, -1)
    for i in range(depth):
        kind, j = i % 3, i // 3
        g_pre = row(norm_g[i, 0])
        if kind == 0:
            w_in = gdn_w_in[j]
            n_main = 4 * HEADS * HEAD_DV
            wc, wr = _gate_weights(w_in[:, n_main:n_main + HEADS], w_in[:, n_main + HEADS:])
            proj, ga, gat = _in_proj(x2, g_pre, w_in[:, :n_main].astype(BF16), wc, wr)
            o16 = _gdn_core(proj, ga, gat, gdn_conv[j], gdn_a_log[j], gdn_dt_bias[j],
                            gdn_norm_g[j], batch=batch, seq=seq)
            w_out = gdn_w_out[j]
        elif kind == 1:
            w_in = mlstm_w_in[j]
            n_main = 3 * HEADS * HEAD_DV
            wc, wr = _gate_weights(w_in[:, n_main:n_main + HEADS], w_in[:, n_main + HEADS:])
            proj, ga, gat = _in_proj(x2, g_pre, w_in[:, :n_main].astype(BF16), wc, wr)
            o16 = _mlstm_core(proj, ga, gat, mlstm_gate_b[j], mlstm_norm_g[j],
                              batch=batch, seq=seq)
            w_out = mlstm_w_out[j]
        else:
            lambda_init = 0.8 - 0.6 * math.exp(-0.3 * i)
            proj = _in_proj(x2, g_pre, diff_w_in[j].astype(BF16))
            o16 = _diff_core(proj, tables, diff_lambda[j], diff_norm_g[j], lambda_init,
                             batch=batch, seq=seq)
            w_out = diff_w_out[j]
        x2 = _out_proj_mlp(x2, o16, w_out.astype(BF16), row(norm_g[i, 1]), row(norm_g[i, 2]),
                           mlp_w1[i].astype(BF16), mlp_w2[i].astype(BF16), row(norm_g[i, 3]))
    return x2.reshape(batch, seq, d)
```
